```python
import jax, jax.numpy as jnp
from jax import lax
import numpy as np

D_MODEL = 2048
BATCH = 4
SEQ = 2048
DEPTH = 4
DEC_BATCH = 8
DEC_SEQ = 4
PAST_LEN = 16384
PAGE_SIZE = 128

N_META = 16
H_ATT = 8
HEAD_DIM = 128
D_ATT = H_ATT * HEAD_DIM
D_CONV = D_MODEL // 4
CONV_WIDTH = 31
D_LRU = D_MODEL // 4
LRU_BLOCKS = 4
LRU_BLOCK = D_LRU // LRU_BLOCKS
LRU_CONV_WIDTH = 4
LRU_C = 8.0
D_MIX = D_ATT + D_CONV + D_LRU
D_IN = 3 * D_ATT + 2 * D_CONV + 2 * D_LRU
D_FF = -(-8 * D_MODEL // (3 * 256)) * 256
Q_BLOCK = 128
SB_BIAS_INIT = -6.0
EPS = 1e-6

kernel_name = 'hymba_sb_conformer_rglru_step'


def rms_norm(x, g):
    xf = x.astype(jnp.float32)
    y = xf * lax.rsqrt(jnp.mean(xf * xf, axis=-1, keepdims=True) + EPS)
    return (y * g.astype(jnp.float32)).astype(x.dtype)


def layer_norm(x, g, b):
    xf = x.astype(jnp.float32)
    xc = xf - jnp.mean(xf, axis=-1, keepdims=True)
    y = xc * lax.rsqrt(jnp.mean(xc * xc, axis=-1, keepdims=True) + EPS)
    return (y * g.astype(jnp.float32) + b.astype(jnp.float32)).astype(x.dtype)


def causal_dwconv(x_ext, w, b):
    c = x_ext.shape[-1]
    y = lax.conv_general_dilated(x_ext, w[:, None, :].astype(x_ext.dtype), window_strides=(1,),
                                 padding='VALID', dimension_numbers=('NWC', 'WIO', 'NWC'),
                                 feature_group_count=c)
    return y + b.astype(x_ext.dtype)


def stick_breaking(q, k, v, bias, q_pos, k_pos):
    z = jnp.einsum('bqhd,bkhd->bhqk', q, k, preferred_element_type=jnp.float32) * (HEAD_DIM ** -0.5)
    z = z + bias.astype(jnp.float32)[None, :, None, None]
    mask = k_pos[None, :] < q_pos[:, None]
    log_beta = jax.nn.log_sigmoid(z)
    log_rest = jnp.where(mask, log_beta - z, 0.0)
    later = lax.cumsum(log_rest, axis=3, reverse=True) - log_rest
    a = jnp.where(mask, jnp.exp(log_beta + later), 0.0)
    return jnp.einsum('bhqk,bkhd->bqhd', a.astype(v.dtype), v)


def sb_prompt(q, k, v, bias):
    t = q.shape[1]
    bounds = [0] + list(range(N_META, t + 1, Q_BLOCK))
    pos = jnp.arange(t)
    outs = [stick_breaking(q[:, qs:qe], k[:, :qe], v[:, :qe], bias, pos[qs:qe], pos[:qe])
            for qs, qe in zip(bounds[:-1], bounds[1:])]
    return jnp.concatenate(outs, axis=1)


def conformer_conv(a, gate, prefix, w, b, ln_g, ln_b):
    u = a * jax.nn.sigmoid(gate)
    ext = jnp.concatenate([prefix.astype(u.dtype), u], axis=1)
    y = jax.nn.silu(layer_norm(causal_dwconv(ext, w, b), ln_g, ln_b))
    return y, ext[:, -(CONV_WIDTH - 1):]


def griffin_rglru(xb, gate, conv_prefix, h0, cw, cb, wa, ba, wx, bx, lam):
    ext = jnp.concatenate([conv_prefix.astype(xb.dtype), xb], axis=1)
    xc = causal_dwconv(ext, cw, cb)
    bsz, t = xc.shape[:2]
    xh = xc.reshape(bsz, t, LRU_BLOCKS, LRU_BLOCK)
    r = jax.nn.sigmoid(jnp.einsum('btni,nij->btnj', xh, wa).reshape(bsz, t, D_LRU) + ba)
    i = jax.nn.sigmoid(jnp.einsum('btni,nij->btnj', xh, wx).reshape(bsz, t, D_LRU) + bx)
    log_a = (-LRU_C * r.astype(jnp.float32)) * jax.nn.softplus(-lam.astype(jnp.float32))
    a = jnp.exp(log_a)
    u = jnp.sqrt(-jnp.expm1(2.0 * log_a)) * (i * xc).astype(jnp.float32)

    def step(h, au):
        h = au[0] * h + au[1]
        return h, h

    h_last, hs = lax.scan(step, h0.astype(jnp.float32),
                          (jnp.swapaxes(a, 0, 1), jnp.swapaxes(u, 0, 1)))
    y = jnp.swapaxes(hs, 0, 1).astype(xb.dtype) * jax.nn.gelu(gate)
    return y, ext[:, -(LRU_CONV_WIDTH - 1):], h_last.astype(h0.dtype)


def trunk_layer(x, attn_fn, conv_prefix, lru_prefix, h0,
                norm_mix, w_in, sb_bias, conv_w, conv_b, conv_ln_g, conv_ln_b, lru_conv_w, lru_conv_b,
                lru_wa, lru_ba, lru_wx, lru_bx, lru_lambda, attn_out_g, lru_out_g, w_out,
                norm_ffn, w_gate, w_up, w_down):
    bsz, t = x.shape[:2]
    p = rms_norm(x, norm_mix) @ w_in
    q = p[..., :D_ATT].reshape(bsz, t, H_ATT, HEAD_DIM)
    k = p[..., D_ATT:2 * D_ATT].reshape(bsz, t, H_ATT, HEAD_DIM)
    v = p[..., 2 * D_ATT:3 * D_ATT].reshape(bsz, t, H_ATT, HEAD_DIM)
    o_att = attn_fn(q, k, v, sb_bias).reshape(bsz, t, D_ATT)
    c0 = 3 * D_ATT
    o_conv, conv_state = conformer_conv(p[..., c0:c0 + D_CONV], p[..., c0 + D_CONV:c0 + 2 * D_CONV],
                                        conv_prefix, conv_w, conv_b, conv_ln_g, conv_ln_b)
    r0 = c0 + 2 * D_CONV
    o_lru, lru_state, h_last = griffin_rglru(p[..., r0:r0 + D_LRU], p[..., r0 + D_LRU:r0 + 2 * D_LRU],
                                             lru_prefix, h0, lru_conv_w, lru_conv_b,
                                             lru_wa, lru_ba, lru_wx, lru_bx, lru_lambda)
    mixed = jnp.concatenate([rms_norm(o_att, attn_out_g), o_conv, rms_norm(o_lru, lru_out_g)], axis=-1)
    x = x + mixed @ w_out
    hdn = rms_norm(x, norm_ffn)
    x = x + (jax.nn.silu(hdn @ w_gate) * (hdn @ w_up)) @ w_down
    return x, k, v, conv_state, lru_state, h_last


def stack_states(per_layer):
    return [jnp.stack([s[i] for s in per_layer], axis=0) for i in range(5)]


def setup_inputs(seed: int = 0) -> dict:
    key = jax.random.key(seed)
    ks = jax.random.split(key, 32)
    f32 = jnp.float32

    def nrm(k, shape, scale):
        return jax.random.normal(k, shape, f32) * scale

    n_pages = PAST_LEN // PAGE_SIZE
    n_used = DEC_BATCH * n_pages
    n_pool = n_used + max(1, n_used // 4)
    page_table = jax.random.permutation(ks[0], n_pool)[:n_used].reshape(DEC_BATCH, n_pages).astype(jnp.int32)
    a0 = jax.random.uniform(ks[1], (DEPTH, D_LRU), f32, 0.9, 0.999)
    sig = a0 ** (1.0 / LRU_C)
    lru_lambda = jnp.log(sig) - jnp.log1p(-sig)
    return {
        'x_prompt': nrm(ks[2], (BATCH, SEQ, D_MODEL), 1.0),
        'x_sample': nrm(ks[3], (DEC_BATCH, DEC_SEQ, D_MODEL), 1.0),
        'cache_k': nrm(ks[4], (DEPTH, n_pool, PAGE_SIZE, H_ATT, HEAD_DIM), 1.0),
        'cache_v': nrm(ks[5], (DEPTH, n_pool, PAGE_SIZE, H_ATT, HEAD_DIM), 1.0),
        'state_conv': nrm(ks[6], (DEPTH, DEC_BATCH, CONV_WIDTH - 1, D_CONV), 0.5),
        'state_lru_conv': nrm(ks[7], (DEPTH, DEC_BATCH, LRU_CONV_WIDTH - 1, D_LRU), 1.0),
        'state_lru_h': nrm(ks[8], (DEPTH, DEC_BATCH, D_LRU), 0.5),
        'page_table': page_table,
        'meta_tokens': nrm(ks[9], (N_META, D_MODEL), 1.0),
        'norm_mix': 1.0 + nrm(ks[10], (DEPTH, D_MODEL), 0.02),
        'w_in': nrm(ks[11], (DEPTH, D_MODEL, D_IN), D_MODEL ** -0.5),
        'sb_bias': SB_BIAS_INIT + nrm(ks[30], (DEPTH, H_ATT), 0.1),
        'conv_w': nrm(ks[12], (DEPTH, CONV_WIDTH, D_CONV), CONV_WIDTH ** -0.5),
        'conv_b': nrm(ks[13], (DEPTH, D_CONV), 0.02),
        'conv_ln_g': 1.0 + nrm(ks[14], (DEPTH, D_CONV), 0.02),
        'conv_ln_b': nrm(ks[15], (DEPTH, D_CONV), 0.02),
        'lru_conv_w': nrm(ks[16], (DEPTH, LRU_CONV_WIDTH, D_LRU), LRU_CONV_WIDTH ** -0.5),
        'lru_conv_b': nrm(ks[17], (DEPTH, D_LRU), 0.02),
        'lru_wa': nrm(ks[18], (DEPTH, LRU_BLOCKS, LRU_BLOCK, LRU_BLOCK), LRU_BLOCK ** -0.5),
        'lru_ba': nrm(ks[19], (DEPTH, D_LRU), 0.02),
        'lru_wx': nrm(ks[20], (DEPTH, LRU_BLOCKS, LRU_BLOCK, LRU_BLOCK), LRU_BLOCK ** -0.5),
        'lru_bx': nrm(ks[21], (DEPTH, D_LRU), 0.02),
        'lru_lambda': lru_lambda,
        'attn_out_g': 1.0 + nrm(ks[22], (DEPTH, D_ATT), 0.02),
        'lru_out_g': 1.0 + nrm(ks[23], (DEPTH, D_LRU), 0.02),
        'w_out': nrm(ks[24], (DEPTH, D_MIX, D_MODEL), D_MIX ** -0.5),
        'norm_ffn': 1.0 + nrm(ks[25], (DEPTH, D_MODEL), 0.02),
        'w_gate': nrm(ks[26], (DEPTH, D_MODEL, D_FF), D_MODEL ** -0.5),
        'w_up': nrm(ks[27], (DEPTH, D_MODEL, D_FF), D_MODEL ** -0.5),
        'w_down': nrm(ks[28], (DEPTH, D_FF, D_MODEL), D_FF ** -0.5),
        'final_norm': 1.0 + nrm(ks[29], (D_MODEL,), 0.02),
    }


def reference(x_prompt, x_sample, cache_k, cache_v, state_conv, state_lru_conv, state_lru_h, page_table,
              meta_tokens, norm_mix, w_in, sb_bias, conv_w, conv_b, conv_ln_g, conv_ln_b, lru_conv_w, lru_conv_b,
              lru_wa, lru_ba, lru_wx, lru_bx, lru_lambda, attn_out_g, lru_out_g, w_out, norm_ffn,
              w_gate, w_up, w_down, final_norm):
    stacked = (norm_mix, w_in, sb_bias, conv_w, conv_b, conv_ln_g, conv_ln_b, lru_conv_w, lru_conv_b,
               lru_wa, lru_ba, lru_wx, lru_bx, lru_lambda, attn_out_g, lru_out_g, w_out,
               norm_ffn, w_gate, w_up, w_down)
    bp = x_prompt.shape[0]
    bs = x_sample.shape[0]
    meta = jnp.broadcast_to(meta_tokens[None].astype(x_prompt.dtype), (bp, N_META, D_MODEL))
    xp = jnp.concatenate([meta, x_prompt], axis=1)
    xs = x_sample
    zero_conv = jnp.zeros((bp, CONV_WIDTH - 1, D_CONV), xp.dtype)
    zero_lru = jnp.zeros((bp, LRU_CONV_WIDTH - 1, D_LRU), xp.dtype)
    zero_h = jnp.zeros((bp, D_LRU), jnp.float32)
    prompt_states = []
    sample_states = []
    for l in range(DEPTH):
        lw = [w[l] for w in stacked]
        xp, *st_p = trunk_layer(xp, sb_prompt, zero_conv, zero_lru, zero_h, *lw)
        k_past = cache_k[l][page_table].reshape(bs, -1, H_ATT, HEAD_DIM)
        v_past = cache_v[l][page_table].reshape(bs, -1, H_ATT, HEAD_DIM)

        def sb_sample(q, k, v, bias, k_past=k_past, v_past=v_past):
            past = k_past.shape[1]
            k_all = jnp.concatenate([k_past.astype(k.dtype), k], axis=1)
            v_all = jnp.concatenate([v_past.astype(v.dtype), v], axis=1)
            pos = jnp.arange(past + k.shape[1])
            return stick_breaking(q, k_all, v_all, bias, pos[past:], pos)

        xs, *st_s = trunk_layer(xs, sb_sample, state_conv[l], state_lru_conv[l], state_lru_h[l], *lw)
        prompt_states.append(st_p)
        sample_states.append(st_s)
    kp, vp, cp, lcp, hp = stack_states(prompt_states)
    ksm, vsm, csm, lcsm, hsm = stack_states(sample_states)
    y_prompt = rms_norm(xp, final_norm)[:, N_META:]
    y_sample = rms_norm(xs, final_norm)
    return (y_prompt, y_sample, kp, vp, cp, lcp, hp, ksm, vsm, csm, lcsm, hsm)
```

```python
import functools

import jax
import jax.numpy as jnp
from jax import lax
from jax.experimental import pallas as pl
from jax.experimental.pallas import tpu as pltpu

F32 = jnp.float32
BF16 = jnp.bfloat16

EPS = 1e-6
LRU_C = 8.0
LANES = 128
KEY_TILE = 128
SAMPLE_ROWS = 16
CONV_PAD = 32
LRU_PAD = 8
VMEM_LIMIT_BYTES = 56 * 1024 * 1024


def _params(*sem):
    return pltpu.CompilerParams(dimension_semantics=sem, vmem_limit_bytes=VMEM_LIMIT_BYTES)


def _row_tile(m, cap):
    if m <= cap:
        return m
    best = None
    for t in range(16, cap + 1, 16):
        if m % t == 0:
            best = t
    assert best is not None, m
    return best


def _rms(x, g):
    return x * lax.rsqrt(jnp.mean(x * x, axis=-1, keepdims=True) + EPS) * g


def _softplus(x):
    return jnp.maximum(x, 0.0) + jnp.log1p(jnp.exp(-jnp.abs(x)))


def _norm_mm_body(x_ref, g_ref, w_ref, o_ref, u_ref):
    @pl.when(pl.program_id(1) == 0)
    def _():
        u_ref[...] = _rms(x_ref[...], g_ref[...]).astype(BF16)

    o_ref[...] = jnp.dot(u_ref[...], w_ref[...], preferred_element_type=F32).astype(o_ref.dtype)


def _norm_matmul(x, g, w, layer, col0, ncols, tn, out_dtype):
    m, k = x.shape
    tm = _row_tile(m, 688)
    off = col0 // tn
    assert col0 % tn == 0 and ncols % tn == 0
    return pl.pallas_call(
        _norm_mm_body,
        grid=(m // tm, ncols // tn),
        in_specs=[
            pl.BlockSpec((tm, k), lambda i, j: (i, 0)),
            pl.BlockSpec((None, 1, k), lambda i, j: (layer, 0, 0)),
            pl.BlockSpec((None, k, tn), lambda i, j: (layer, 0, j + off)),
        ],
        out_specs=pl.BlockSpec((tm, tn), lambda i, j: (i, j)),
        out_shape=jax.ShapeDtypeStruct((m, ncols), out_dtype),
        scratch_shapes=[pltpu.VMEM((tm, k), BF16)],
        compiler_params=_params("parallel", "arbitrary"),
    )(x, g, w)


def _norm_gateup_body(x_ref, g_ref, wg_ref, wu_ref, o_ref, u_ref):
    @pl.when(pl.program_id(1) == 0)
    def _():
        u_ref[...] = _rms(x_ref[...], g_ref[...]).astype(BF16)

    u = u_ref[...]
    gate = jnp.dot(u, wg_ref[...], preferred_element_type=F32)
    up = jnp.dot(u, wu_ref[...], preferred_element_type=F32)
    o_ref[...] = (gate * jax.nn.sigmoid(gate) * up).astype(o_ref.dtype)


def _norm_gateup(x, g, wg, wu, layer, tn):
    m, k = x.shape
    n = wg.shape[-1]
    tm = _row_tile(m, 688)
    return pl.pallas_call(
        _norm_gateup_body,
        grid=(m // tm, n // tn),
        in_specs=[
            pl.BlockSpec((tm, k), lambda i, j: (i, 0)),
            pl.BlockSpec((None, 1, k), lambda i, j: (layer, 0, 0)),
            pl.BlockSpec((None, k, tn), lambda i, j: (layer, 0, j)),
            pl.BlockSpec((None, k, tn), lambda i, j: (layer, 0, j)),
        ],
        out_specs=pl.BlockSpec((tm, tn), lambda i, j: (i, j)),
        out_shape=jax.ShapeDtypeStruct((m, n), BF16),
        scratch_shapes=[pltpu.VMEM((tm, k), BF16)],
        compiler_params=_params("parallel", "arbitrary"),
    )(x, g, wg, wu)


def _res_mm_body(r_ref, a_ref, w_ref, o_ref):
    o_ref[...] = r_ref[...] + jnp.dot(a_ref[...], w_ref[...], preferred_element_type=F32)


def _res_matmul(res, a, w, layer, tn):
    m, k = a.shape
    n = w.shape[-1]
    tm = _row_tile(m, 688)
    return pl.pallas_call(
        _res_mm_body,
        grid=(m // tm, n // tn),
        in_specs=[
            pl.BlockSpec((tm, tn), lambda i, j: (i, j)),
            pl.BlockSpec((tm, k), lambda i, j: (i, 0)),
            pl.BlockSpec((None, k, tn), lambda i, j: (layer, 0, j)),
        ],
        out_specs=pl.BlockSpec((tm, tn), lambda i, j: (i, j)),
        out_shape=jax.ShapeDtypeStruct((m, n), F32),
        compiler_params=_params("parallel", "arbitrary"),
    )(res, a, w)


def _mix_out_body(r_ref, att_ref, ag_ref, conv_ref, lru_ref, wa_ref, wc_ref, wl_ref, o_ref, an_ref):
    @pl.when(pl.program_id(1) == 0)
    def _():
        an_ref[...] = _rms(att_ref[...], ag_ref[...]).astype(BF16)

    acc = jnp.dot(an_ref[...], wa_ref[...], preferred_element_type=F32)
    acc += jnp.dot(conv_ref[...], wc_ref[...], preferred_element_type=F32)
    acc += jnp.dot(lru_ref[...], wl_ref[...], preferred_element_type=F32)
    o_ref[...] = r_ref[...] + acc


def _mix_out(res, att, att_g, conv, lru, w, layer, tn):
    m, n = res.shape
    d_att, d_conv, d_lru = att.shape[1], conv.shape[1], lru.shape[1]
    tm = _row_tile(m, 688)
    assert d_att % d_conv == 0 and d_conv == d_lru
    c_blk = d_att // d_conv
    return pl.pallas_call(
        _mix_out_body,
        grid=(m // tm, n // tn),
        in_specs=[
            pl.BlockSpec((tm, tn), lambda i, j: (i, j)),
            pl.BlockSpec((tm, d_att), lambda i, j: (i, 0)),
            pl.BlockSpec((None, 1, d_att), lambda i, j: (layer, 0, 0)),
            pl.BlockSpec((tm, d_conv), lambda i, j: (i, 0)),
            pl.BlockSpec((tm, d_lru), lambda i, j: (i, 0)),
            pl.BlockSpec((None, d_att, tn), lambda i, j: (layer, 0, j)),
            pl.BlockSpec((None, d_conv, tn), lambda i, j: (layer, c_blk, j)),
            pl.BlockSpec((None, d_lru, tn), lambda i, j: (layer, c_blk + 1, j)),
        ],
        out_specs=pl.BlockSpec((tm, tn), lambda i, j: (i, j)),
        out_shape=jax.ShapeDtypeStruct((m, n), F32),
        scratch_shapes=[pltpu.VMEM((tm, d_att), BF16)],
        compiler_params=_params("parallel", "arbitrary"),
    )(res, att, att_g, conv, lru, w, w, w)


def _final_norm_body(x_ref, g_ref, o_ref):
    o_ref[...] = _rms(x_ref[...], g_ref[...])


def _final_norm(x, g):
    m, d = x.shape
    tm = _row_tile(m, 512)
    return pl.pallas_call(
        _final_norm_body,
        grid=(m // tm,),
        in_specs=[pl.BlockSpec((tm, d), lambda i: (i, 0)), pl.BlockSpec((1, d), lambda i: (0, 0))],
        out_specs=pl.BlockSpec((tm, d), lambda i: (i, 0)),
        out_shape=jax.ShapeDtypeStruct((m, d), F32),
        compiler_params=_params("parallel"),
    )(x, g)


def _suffix_matrix():
    s = jnp.arange(KEY_TILE)[:, None]
    j = jnp.arange(KEY_TILE)[None, :]
    half = jnp.concatenate([(s > j).astype(BF16), jnp.ones((KEY_TILE, KEY_TILE), BF16)], axis=1)
    return jnp.concatenate([half, half], axis=0)


def _sb_tile(z, mask, suffix, carry):
    log_beta = jnp.minimum(z, 0.0) - jnp.log1p(jnp.exp(-jnp.abs(z)))
    log_rest = log_beta - z
    if mask is not None:
        log_rest = jnp.where(mask, log_rest, 0.0)
    hi = log_rest.astype(BF16)
    lo = (log_rest - hi.astype(F32)).astype(BF16)
    sums = jnp.dot(jnp.concatenate([hi, lo], axis=1), suffix, preferred_element_type=F32)
    a = jnp.exp(log_beta + carry + sums[:, :KEY_TILE])
    if mask is not None:
        a = jnp.where(mask, a, 0.0)
    return a, carry + sums[:, KEY_TILE:]


def _sb_prompt_body(bias_ref, q_ref, k_ref, v_ref, suf_ref, o_ref, kb_ref, vb_ref, *, n_meta, n_chunks, scale):
    bias = bias_ref[pl.program_id(1)]
    kb_ref[...] = k_ref[...].astype(BF16)
    vb_ref[...] = v_ref[...].astype(BF16)
    suffix = suf_ref[...]
    row = lax.broadcasted_iota(jnp.int32, (KEY_TILE, KEY_TILE), 0)
    col = lax.broadcasted_iota(jnp.int32, (KEY_TILE, KEY_TILE), 1)
    zero = jnp.zeros((KEY_TILE, KEY_TILE), F32)

    def tile(qb, ks, mask, carry, acc):
        kt = kb_ref[pl.ds(ks, KEY_TILE), :]
        vt = vb_ref[pl.ds(ks, KEY_TILE), :]
        s = lax.dot_general(qb, kt, (((1,), (1,)), ((), ())), preferred_element_type=F32)
        a, carry = _sb_tile(s * scale + bias, mask, suffix, carry)
        acc = acc + jnp.dot(a.astype(BF16), vt, preferred_element_type=F32)
        return carry, acc

    def q_chunk(c, _):
        qs = pl.multiple_of(n_meta + c * KEY_TILE, 16)
        qb = q_ref[pl.ds(qs, KEY_TILE), :]
        carry, acc = tile(qb, qs, row > col, zero, zero)

        def k_step(i, ca):
            ks = pl.multiple_of(n_meta + (c - 1 - i) * KEY_TILE, 16)
            return tile(qb, ks, None, *ca)

        carry, acc = lax.fori_loop(0, c, k_step, (carry, acc))
        _, acc = tile(qb, 0, col < n_meta, carry, acc)
        o_ref[pl.ds(qs, KEY_TILE), :] = acc
        return 0

    lax.fori_loop(0, n_chunks, q_chunk, 0)
    _, acc = tile(q_ref[pl.ds(0, KEY_TILE), :], 0, (row > col) & (col < n_meta), zero, zero)
    o_ref[pl.ds(0, n_meta), :] = acc[:n_meta]


def _sb_prompt(q, k, v, bias, suffix, layer, bsz, t, n_meta, head_dim):
    m, d_att = k.shape
    n_heads = d_att // head_dim
    assert head_dim == LANES and n_meta % 16 == 0 and (t - n_meta) % KEY_TILE == 0 and t >= KEY_TILE
    body = functools.partial(_sb_prompt_body, n_meta=n_meta, n_chunks=(t - n_meta) // KEY_TILE,
                             scale=head_dim ** -0.5)
    blk = pl.BlockSpec((t, head_dim), lambda b, h: (b, h))
    return pl.pallas_call(
        body,
        grid=(bsz, n_heads),
        in_specs=[
            pl.BlockSpec(memory_space=pltpu.SMEM),
            blk, blk, blk,
            pl.BlockSpec((2 * KEY_TILE, 2 * KEY_TILE), lambda b, h: (0, 0)),
        ],
        out_specs=blk,
        out_shape=jax.ShapeDtypeStruct((m, d_att), F32),
        scratch_shapes=[pltpu.VMEM((t, head_dim), BF16), pltpu.VMEM((t, head_dim), BF16)],
        compiler_params=_params("parallel", "parallel"),
    )(bias[layer], q, k, v, suffix)


def _sb_sample_body(pt_ref, bias_ref, q_ref, kn_ref, vn_ref, kp_ref, vp_ref, suf_ref, o_ref, carry_ref,
                    *, n_heads, head_dim, scale):
    j = pl.program_id(1)
    rows = n_heads * SAMPLE_ROWS
    suffix = suf_ref[...]

    @pl.when(j == 0)
    def _():
        o_ref[...] = jnp.zeros_like(o_ref)
        carry_ref[...] = jnp.zeros_like(carry_ref)

    def process(k_ref, v_ref, mask):
        zs = []
        for h in range(n_heads):
            kh = k_ref[pl.ds(h, KEY_TILE, stride=n_heads), :].astype(BF16)
            s = lax.dot_general(q_ref[h], kh, (((1,), (1,)), ((), ())), preferred_element_type=F32)
            zs.append(s * scale + bias_ref[h])
        a, carry = _sb_tile(jnp.concatenate(zs, axis=0), mask, suffix, carry_ref[...])
        carry_ref[...] = carry
        a = a.astype(BF16)
        for h in range(n_heads):
            vh = v_ref[pl.ds(h, KEY_TILE, stride=n_heads), :].astype(BF16)
            o_ref[h] += jnp.dot(a[h * SAMPLE_ROWS:(h + 1) * SAMPLE_ROWS], vh, preferred_element_type=F32)

    @pl.when(j == 0)
    def _():
        row = lax.broadcasted_iota(jnp.int32, (rows, KEY_TILE), 0) % SAMPLE_ROWS
        col = lax.broadcasted_iota(jnp.int32, (rows, KEY_TILE), 1)
        process(kn_ref, vn_ref, col < row)

    @pl.when(j > 0)
    def _():
        process(kp_ref, vp_ref, None)


def _sb_sample(q, k_new, v_new, cache_k, cache_v, page_table, bias, suffix, layer, head_dim):
    bd, n_heads = q.shape[:2]
    n_pages = page_table.shape[1]
    page, d_att = cache_k.shape[2] // n_heads, head_dim
    assert page == KEY_TILE and head_dim == LANES
    page = page * n_heads
    body = functools.partial(_sb_sample_body, n_heads=n_heads, head_dim=head_dim, scale=head_dim ** -0.5)

    def page_map(b, j, pt):
        return (layer, pt[b, n_pages - jnp.maximum(j, 1)], 0, 0)

    new_blk = pl.BlockSpec((None, page, d_att), lambda b, j, pt: (b, 0, 0))
    page_blk = pl.BlockSpec((None, None, page, d_att), page_map)
    qo_blk = pl.BlockSpec((None, n_heads, SAMPLE_ROWS, head_dim), lambda b, j, pt: (b, 0, 0, 0))
    return pl.pallas_call(
        body,
        grid_spec=pltpu.PrefetchScalarGridSpec(
            num_scalar_prefetch=1,
            grid=(bd, n_pages + 1),
            in_specs=[
                pl.BlockSpec(memory_space=pltpu.SMEM),
                qo_blk, new_blk, new_blk, page_blk, page_blk,
                pl.BlockSpec((2 * KEY_TILE, 2 * KEY_TILE), lambda b, j, pt: (0, 0)),
            ],
            out_specs=qo_blk,
            scratch_shapes=[pltpu.VMEM((n_heads * SAMPLE_ROWS, KEY_TILE), F32)],
        ),
        out_shape=jax.ShapeDtypeStruct((bd, n_heads, SAMPLE_ROWS, head_dim), F32),
        compiler_params=_params("parallel", "arbitrary"),
    )(page_table, bias[layer], q, k_new, v_new, cache_k, cache_v, suffix)


def _conv_body(a_ref, g_ref, pre_ref, w_ref, b_ref, lg_ref, lb_ref, o_ref, st_ref, ext_ref, sh_ref,
               *, t, t_valid, width, chunk):
    base = CONV_PAD - (width - 1)
    ext_ref[pl.ds(base, width - 1), :] = pre_ref[...]
    n_chunks = t // chunk

    def glu(i, _):
        r = pl.ds(pl.multiple_of(i * chunk, 8), chunk)
        u = a_ref[r, :] * jax.nn.sigmoid(g_ref[r, :])
        ext_ref[pl.ds(pl.multiple_of(CONV_PAD + i * chunk, 8), chunk), :] = u
        return 0

    lax.fori_loop(0, n_chunks, glu, 0)
    st_ref[...] = ext_ref[pl.ds(CONV_PAD + t_valid - (width - 1), width - 1), :]

    def conv(i, _):
        t0 = pl.multiple_of(i * chunk, 8)
        win = ext_ref[pl.ds(t0, chunk + CONV_PAD), :]
        for r in range(1, 8):
            sh_ref[r] = win[r:r + chunk + CONV_PAD - 8]
        acc = jnp.zeros((chunk, a_ref.shape[-1]), F32) + b_ref[...]
        for w in range(width):
            q, r = divmod(base + w, 8)
            if r == 0:
                tap = ext_ref[pl.ds(pl.multiple_of(t0 + 8 * q, 8), chunk), :]
            else:
                tap = sh_ref[r, pl.ds(8 * q, chunk), :]
            acc = acc + tap * w_ref[pl.ds(w, 1), :]
        mean = jnp.mean(acc, axis=-1, keepdims=True)
        cen = acc - mean
        y = cen * lax.rsqrt(jnp.mean(cen * cen, axis=-1, keepdims=True) + EPS) * lg_ref[...] + lb_ref[...]
        o_ref[pl.ds(t0, chunk), :] = (y * jax.nn.sigmoid(y)).astype(o_ref.dtype)
        return 0

    lax.fori_loop(0, n_chunks, conv, 0)


def _conv_branch(p, col_blk, prefix, w, b, ln_g, ln_b, layer, n_seq, t, t_valid):
    width, c = w.shape[1:]
    chunk = _row_tile(t, 48)
    assert t % chunk == 0 and width - 1 <= CONV_PAD and 0 < t_valid <= t
    body = functools.partial(_conv_body, t=t, t_valid=t_valid, width=width, chunk=chunk)
    vec = pl.BlockSpec((None, 1, c), lambda s: (layer, 0, 0))
    return pl.pallas_call(
        body,
        grid=(n_seq,),
        in_specs=[
            pl.BlockSpec((t, c), lambda s: (s, col_blk)),
            pl.BlockSpec((t, c), lambda s: (s, col_blk + 1)),
            pl.BlockSpec((None, width - 1, c), lambda s: (s, 0, 0)),
            pl.BlockSpec((None, width, c), lambda s: (layer, 0, 0)),
            vec, vec, vec,
        ],
        out_specs=[
            pl.BlockSpec((t, c), lambda s: (s, 0)),
            pl.BlockSpec((None, width - 1, c), lambda s: (s, 0, 0)),
        ],
        out_shape=[
            jax.ShapeDtypeStruct((n_seq * t, c), BF16),
            jax.ShapeDtypeStruct((n_seq, width - 1, c), F32),
        ],
        scratch_shapes=[pltpu.VMEM((CONV_PAD + t, c), F32), pltpu.VMEM((8, chunk + CONV_PAD - 8, c), F32)],
        compiler_params=_params("parallel"),
    )(p, p, prefix, w, b, ln_g, ln_b)


def _lru_body(x_ref, gate_ref, pre_ref, h0_ref, cw_ref, cb_ref, wa_ref, ba_ref, wx_ref, bx_ref, lam_ref, og_ref,
              o_ref, st_ref, hl_ref, ext_ref, a_ref, u_ref, *, t, t_valid, width, chunk):
    base = LRU_PAD - (width - 1)
    ext_ref[pl.ds(base, width - 1), :] = pre_ref[...]
    n_chunks = t // chunk

    def fill(i, _):
        t0 = pl.multiple_of(i * chunk, 8)
        ext_ref[pl.ds(LRU_PAD + t0, chunk), :] = x_ref[pl.ds(t0, chunk), :]
        return 0

    lax.fori_loop(0, n_chunks, fill, 0)
    st_ref[...] = ext_ref[pl.ds(LRU_PAD + t_valid - (width - 1), width - 1), :]
    neg_c_sp = -LRU_C * _softplus(-lam_ref[...])

    def gates(i, _):
        t0 = pl.multiple_of(i * chunk, 8)
        win = ext_ref[pl.ds(t0, chunk + LRU_PAD), :]
        xc = jnp.zeros((chunk, x_ref.shape[-1]), F32) + cb_ref[...]
        for w in range(width):
            xc = xc + win[base + w:base + w + chunk] * cw_ref[pl.ds(w, 1), :]
        xb = xc.astype(BF16)
        r = jax.nn.sigmoid(jnp.dot(xb, wa_ref[...], preferred_element_type=F32) + ba_ref[...])
        ig = jax.nn.sigmoid(jnp.dot(xb, wx_ref[...], preferred_element_type=F32) + bx_ref[...])
        log_a = r * neg_c_sp
        a = jnp.exp(log_a)
        a_ref[pl.ds(t0, chunk), :] = a
        u_ref[pl.ds(t0, chunk), :] = jnp.sqrt(-jnp.tanh(log_a) * (1.0 + a * a)) * (ig * xc)
        return 0

    lax.fori_loop(0, n_chunks, gates, 0)

    def step(s, h):
        h = a_ref[pl.ds(s, 1), :] * h + u_ref[pl.ds(s, 1), :]
        u_ref[pl.ds(s, 1), :] = h
        return h

    hl_ref[...] = lax.fori_loop(0, t_valid, step, h0_ref[...], unroll=min(8, t_valid))

    def out(i, _):
        r = pl.ds(pl.multiple_of(i * chunk, 8), chunk)
        y = u_ref[r, :] * jax.nn.gelu(gate_ref[r, :])
        o_ref[r, :] = _rms(y, og_ref[...]).astype(o_ref.dtype)
        return 0

    lax.fori_loop(0, n_chunks, out, 0)


def _lru_branch(p, col_blk, prefix, h0, cw, cb, wa, ba, wx, bx, lam, out_g, layer, n_seq, t, t_valid, chunk):
    width, c = cw.shape[1:]
    assert t % chunk == 0 and chunk % 8 == 0 and width - 1 <= LRU_PAD
    body = functools.partial(_lru_body, t=t, t_valid=t_valid, width=width, chunk=chunk)
    vec = pl.BlockSpec((None, 1, c), lambda s: (layer, 0, 0))
    mat = pl.BlockSpec((None, c, c), lambda s: (layer, 0, 0))
    return pl.pallas_call(
        body,
        grid=(n_seq,),
        in_specs=[
            pl.BlockSpec((t, c), lambda s: (s, col_blk)),
            pl.BlockSpec((t, c), lambda s: (s, col_blk + 1)),
            pl.BlockSpec((None, width - 1, c), lambda s: (s, 0, 0)),
            pl.BlockSpec((None, 1, c), lambda s: (s, 0, 0)),
            pl.BlockSpec((None, width, c), lambda s: (layer, 0, 0)),
            vec, mat, vec, mat, vec, vec, vec,
        ],
        out_specs=[
            pl.BlockSpec((t, c), lambda s: (s, 0)),
            pl.BlockSpec((None, width - 1, c), lambda s: (s, 0, 0)),
            pl.BlockSpec((None, 1, c), lambda s: (s, 0, 0)),
        ],
        out_shape=[
            jax.ShapeDtypeStruct((n_seq * t, c), BF16),
            jax.ShapeDtypeStruct((n_seq, width - 1, c), F32),
            jax.ShapeDtypeStruct((n_seq, 1, c), F32),
        ],
        scratch_shapes=[pltpu.VMEM((LRU_PAD + t, c), F32), pltpu.VMEM((t, c), F32), pltpu.VMEM((t, c), F32)],
        compiler_params=_params("parallel"),
    )(p, p, prefix, h0, cw, cb, wa, ba, wx, bx, lam, out_g)


def _block_diag(w):
    depth, nb, bi, bj = w.shape
    eye = jnp.eye(nb, dtype=w.dtype)
    return jnp.einsum("lnij,nm->lnimj", w, eye).reshape(depth, nb * bi, nb * bj)


def kernel(x_prompt, x_sample, cache_k, cache_v, state_conv, state_lru_conv, state_lru_h, page_table, meta_tokens, norm_mix, w_in, sb_bias, conv_w, conv_b, conv_ln_g, conv_ln_b, lru_conv_w, lru_conv_b, lru_wa, lru_ba, lru_wx, lru_bx, lru_lambda, attn_out_g, lru_out_g, w_out, norm_ffn, w_gate, w_up, w_down, final_norm):
    bp, seq, d_model = x_prompt.shape
    bs, dec_seq, _ = x_sample.shape
    depth, n_pool, page, n_heads, head_dim = cache_k.shape
    n_meta = meta_tokens.shape[0]
    d_att = n_heads * head_dim
    d_conv = conv_w.shape[-1]
    d_lru = lru_conv_w.shape[-1]
    conv_width = conv_w.shape[1]
    lru_width = lru_conv_w.shape[1]
    t_p = n_meta + seq
    t_s = SAMPLE_ROWS
    assert dec_seq <= SAMPLE_ROWS and d_conv == d_lru and d_att % d_conv == 0
    assert w_in.shape[-1] == 3 * d_att + 2 * d_conv + 2 * d_lru

    w_in_b, w_out_b = w_in.astype(BF16), w_out.astype(BF16)
    w_gate_b, w_up_b, w_down_b = w_gate.astype(BF16), w_up.astype(BF16), w_down.astype(BF16)
    wa_b, wx_b = _block_diag(lru_wa).astype(BF16), _block_diag(lru_wx).astype(BF16)
    row = lambda v: v.reshape(depth, 1, -1)
    norm_mix_r, norm_ffn_r, attn_g_r, lru_g_r = row(norm_mix), row(norm_ffn), row(attn_out_g), row(lru_out_g)
    conv_b_r, ln_g_r, ln_b_r = row(conv_b), row(conv_ln_g), row(conv_ln_b)
    lcb_r, ba_r, bx_r, lam_r = row(lru_conv_b), row(lru_ba), row(lru_bx), row(lru_lambda)
    suffix = _suffix_matrix()
    cache_k2 = cache_k.reshape(depth, n_pool, page * n_heads, head_dim)
    cache_v2 = cache_v.reshape(depth, n_pool, page * n_heads, head_dim)

    meta = jnp.broadcast_to(meta_tokens[None].astype(x_prompt.dtype), (bp, n_meta, d_model))
    xp = jnp.concatenate([meta, x_prompt], axis=1).reshape(bp * t_p, d_model)
    xs = jnp.pad(x_sample, ((0, 0), (0, t_s - dec_seq), (0, 0))).reshape(bs * t_s, d_model)

    zero_conv = jnp.zeros((bp, conv_width - 1, d_conv), F32)
    zero_lru = jnp.zeros((bp, lru_width - 1, d_lru), F32)
    zero_h = jnp.zeros((bp, 1, d_lru), F32)
    conv_blk = 3 * d_att // d_conv
    lru_chunk_p = _row_tile(t_p, 688)

    def mixers(x, layer, n_seq, t, t_valid, conv_pre, lru_pre, h0, lru_chunk):
        q = _norm_matmul(x, norm_mix_r, w_in_b, layer, 0, d_att, d_att, BF16)
        k = _norm_matmul(x, norm_mix_r, w_in_b, layer, d_att, d_att, d_att, F32)
        v = _norm_matmul(x, norm_mix_r, w_in_b, layer, 2 * d_att, d_att, d_att, F32)
        rest = _norm_matmul(x, norm_mix_r, w_in_b, layer, 3 * d_att, 2 * d_conv + 2 * d_lru, d_att, F32)
        o_conv, st_conv = _conv_branch(rest, 0, conv_pre, conv_w, conv_b_r, ln_g_r, ln_b_r, layer, n_seq, t, t_valid)
        o_lru, st_lru, h_last = _lru_branch(rest, 2, lru_pre, h0, lru_conv_w, lcb_r, wa_b, ba_r, wx_b, bx_r,
                                            lam_r, lru_g_r, layer, n_seq, t, t_valid, lru_chunk)
        return q, k, v, o_conv, o_lru, st_conv, st_lru, h_last

    def finish(x, layer, att, o_conv, o_lru):
        x = _mix_out(x, att, attn_g_r, o_conv, o_lru, w_out_b, layer, 512)
        act = _norm_gateup(x, norm_ffn_r, w_gate_b, w_up_b, layer, 512)
        return _res_matmul(x, act, w_down_b, layer, 512)

    prompt_states, sample_states = [], []
    for layer in range(depth):
        q, k, v, o_conv, o_lru, st_conv, st_lru, h_last = mixers(
            xp, layer, bp, t_p, t_p, zero_conv, zero_lru, zero_h, lru_chunk_p)
        att = _sb_prompt(q, k, v, sb_bias, suffix, layer, bp, t_p, n_meta, head_dim)
        xp = finish(xp, layer, att, o_conv, o_lru)
        prompt_states.append((k.reshape(bp, t_p, n_heads, head_dim), v.reshape(bp, t_p, n_heads, head_dim),
                              st_conv, st_lru, h_last.reshape(bp, d_lru)))

        q, k, v, o_conv, o_lru, st_conv, st_lru, h_last = mixers(
            xs, layer, bs, t_s, dec_seq, state_conv[layer], state_lru_conv[layer],
            state_lru_h[layer].reshape(bs, 1, d_lru), t_s)
        q4 = q.reshape(bs, t_s, n_heads, head_dim).transpose(0, 2, 1, 3)
        pad_new = lambda a: jnp.pad(a.reshape(bs, t_s, d_att)[:, :dec_seq], ((0, 0), (0, KEY_TILE - dec_seq), (0, 0))
                                    ).reshape(bs, KEY_TILE * n_heads, head_dim)
        att = _sb_sample(q4, pad_new(k), pad_new(v), cache_k2, cache_v2, page_table, sb_bias, suffix, layer, head_dim)
        att = att.transpose(0, 2, 1, 3).reshape(bs * t_s, d_att)
        xs = finish(xs, layer, att, o_conv, o_lru)
        new = lambda a: a.reshape(bs, t_s, n_heads, head_dim)[:, :dec_seq]
        sample_states.append((new(k), new(v), st_conv, st_lru, h_last.reshape(bs, d_lru)))

    stack = lambda states: [jnp.stack([s[i] for s in states], axis=0) for i in range(5)]
    kp, vp, cp, lcp, hp = stack(prompt_states)
    ksm, vsm, csm, lcsm, hsm = stack(sample_states)
    fin = final_norm.reshape(1, d_model)
    y_prompt = _final_norm(xp.reshape(bp, t_p, d_model)[:, n_meta:].reshape(bp * seq, d_model), fin)
    y_sample = _final_norm(xs, fin).reshape(bs, t_s, d_model)[:, :dec_seq]
    return (y_prompt.reshape(bp, seq, d_model), y_sample, kp, vp, cp, lcp, hp, ksm, vsm, csm, lcsm, hsm)
```

```python
import functools

import jax
import jax.numpy as jnp
from jax import lax
from jax.experimental import pallas as pl
from jax.experimental.pallas import tpu as pltpu

F32 = jnp.float32
BF16 = jnp.bfloat16

EPS = 1e-6
LRU_C = 8.0
LANES = 128
KEY_TILE = 128
SAMPLE_ROWS = 16
CONV_PAD = 32
LRU_PAD = 8
VMEM_LIMIT_BYTES = 56 * 1024 * 1024


def _params(*sem):
    return pltpu.CompilerParams(dimension_semantics=sem, vmem_limit_bytes=VMEM_LIMIT_BYTES)


def _row_tile(m, cap):
    if m <= cap:
        return m
    best = None
    for t in range(16, cap + 1, 16):
        if m % t == 0:
            best = t
    assert best is not None, m
    return best


def _col_tile(n, cap):
    best = max(t for t in range(LANES, min(n, cap) + 1, LANES) if n % t == 0)
    return best


def _rms(x, g):
    return x * lax.rsqrt(jnp.mean(x * x, axis=-1, keepdims=True) + EPS) * g


def _softplus(x):
    return jnp.maximum(x, 0.0) + jnp.log1p(jnp.exp(-jnp.abs(x)))


def _norm_mm_body(x_ref, g_ref, w_ref, o_ref, u_ref):
    @pl.when(pl.program_id(1) == 0)
    def _():
        u_ref[...] = _rms(x_ref[...], g_ref[...]).astype(BF16)

    o_ref[...] = jnp.dot(u_ref[...], w_ref[...], preferred_element_type=F32).astype(o_ref.dtype)


def _norm_matmul(x, g, w, layer, col0, ncols, tn, out_dtype):
    m, k = x.shape
    tm = _row_tile(m, 688)
    off = col0 // tn
    assert col0 % tn == 0 and ncols % tn == 0
    return pl.pallas_call(
        _norm_mm_body,
        grid=(m // tm, ncols // tn),
        in_specs=[
            pl.BlockSpec((tm, k), lambda i, j: (i, 0)),
            pl.BlockSpec((None, 1, k), lambda i, j: (layer, 0, 0)),
            pl.BlockSpec((None, k, tn), lambda i, j: (layer, 0, j + off)),
        ],
        out_specs=pl.BlockSpec((tm, tn), lambda i, j: (i, j)),
        out_shape=jax.ShapeDtypeStruct((m, ncols), out_dtype),
        scratch_shapes=[pltpu.VMEM((tm, k), BF16)],
        compiler_params=_params("parallel", "arbitrary"),
        name="in_proj",
    )(x, g, w)


def _norm_gateup_body(x_ref, g_ref, wg_ref, wu_ref, o_ref, u_ref):
    @pl.when(pl.program_id(1) == 0)
    def _():
        u_ref[...] = _rms(x_ref[...], g_ref[...]).astype(BF16)

    u = u_ref[...]
    gate = jnp.dot(u, wg_ref[...], preferred_element_type=F32)
    up = jnp.dot(u, wu_ref[...], preferred_element_type=F32)
    o_ref[...] = (gate * jax.nn.sigmoid(gate) * up).astype(o_ref.dtype)


def _norm_gateup(x, g, wg, wu, layer, tn):
    m, k = x.shape
    n = wg.shape[-1]
    tm = _row_tile(m, 688)
    return pl.pallas_call(
        _norm_gateup_body,
        grid=(m // tm, n // tn),
        in_specs=[
            pl.BlockSpec((tm, k), lambda i, j: (i, 0)),
            pl.BlockSpec((None, 1, k), lambda i, j: (layer, 0, 0)),
            pl.BlockSpec((None, k, tn), lambda i, j: (layer, 0, j)),
            pl.BlockSpec((None, k, tn), lambda i, j: (layer, 0, j)),
        ],
        out_specs=pl.BlockSpec((tm, tn), lambda i, j: (i, j)),
        out_shape=jax.ShapeDtypeStruct((m, n), BF16),
        scratch_shapes=[pltpu.VMEM((tm, k), BF16)],
        compiler_params=_params("parallel", "arbitrary"),
        name="ffn_gate_up",
    )(x, g, wg, wu)


def _res_mm_body(r_ref, a_ref, w_ref, o_ref):
    o_ref[...] = r_ref[...] + jnp.dot(a_ref[...], w_ref[...], preferred_element_type=F32)


def _res_matmul(res, a, w, layer, tn):
    m, k = a.shape
    n = w.shape[-1]
    tm = _row_tile(m, 688)
    return pl.pallas_call(
        _res_mm_body,
        grid=(m // tm, n // tn),
        in_specs=[
            pl.BlockSpec((tm, tn), lambda i, j: (i, j)),
            pl.BlockSpec((tm, k), lambda i, j: (i, 0)),
            pl.BlockSpec((None, k, tn), lambda i, j: (layer, 0, j)),
        ],
        out_specs=pl.BlockSpec((tm, tn), lambda i, j: (i, j)),
        out_shape=jax.ShapeDtypeStruct((m, n), F32),
        compiler_params=_params("parallel", "arbitrary"),
        name="ffn_down",
    )(res, a, w)


def _mix_out_body(r_ref, att_ref, ag_ref, conv_ref, lru_ref, wa_ref, wc_ref, wl_ref, o_ref, an_ref):
    @pl.when(pl.program_id(1) == 0)
    def _():
        an_ref[...] = _rms(att_ref[...], ag_ref[...]).astype(BF16)

    acc = jnp.dot(an_ref[...], wa_ref[...], preferred_element_type=F32)
    acc += jnp.dot(conv_ref[...], wc_ref[...], preferred_element_type=F32)
    acc += jnp.dot(lru_ref[...], wl_ref[...], preferred_element_type=F32)
    o_ref[...] = r_ref[...] + acc


def _mix_out(res, att, att_g, conv, lru, w, layer, tn):
    m, n = res.shape
    d_att, d_conv, d_lru = att.shape[1], conv.shape[1], lru.shape[1]
    tm = _row_tile(m, 688)
    assert d_att % d_conv == 0 and d_conv == d_lru
    c_blk = d_att // d_conv
    return pl.pallas_call(
        _mix_out_body,
        grid=(m // tm, n // tn),
        in_specs=[
            pl.BlockSpec((tm, tn), lambda i, j: (i, j)),
            pl.BlockSpec((tm, d_att), lambda i, j: (i, 0)),
            pl.BlockSpec((None, 1, d_att), lambda i, j: (layer, 0, 0)),
            pl.BlockSpec((tm, d_conv), lambda i, j: (i, 0)),
            pl.BlockSpec((tm, d_lru), lambda i, j: (i, 0)),
            pl.BlockSpec((None, d_att, tn), lambda i, j: (layer, 0, j)),
            pl.BlockSpec((None, d_conv, tn), lambda i, j: (layer, c_blk, j)),
            pl.BlockSpec((None, d_lru, tn), lambda i, j: (layer, c_blk + 1, j)),
        ],
        out_specs=pl.BlockSpec((tm, tn), lambda i, j: (i, j)),
        out_shape=jax.ShapeDtypeStruct((m, n), F32),
        scratch_shapes=[pltpu.VMEM((tm, d_att), BF16)],
        compiler_params=_params("parallel", "arbitrary"),
        name="mix_out",
    )(res, att, att_g, conv, lru, w, w, w)


def _final_norm_body(x_ref, g_ref, o_ref):
    o_ref[...] = _rms(x_ref[...], g_ref[...])


def _final_norm(x, g):
    m, d = x.shape
    tm = _row_tile(m, 512)
    return pl.pallas_call(
        _final_norm_body,
        grid=(m // tm,),
        in_specs=[pl.BlockSpec((tm, d), lambda i: (i, 0)), pl.BlockSpec((1, d), lambda i: (0, 0))],
        out_specs=pl.BlockSpec((tm, d), lambda i: (i, 0)),
        out_shape=jax.ShapeDtypeStruct((m, d), F32),
        compiler_params=_params("parallel"),
        name="final_norm",
    )(x, g)


def _suffix_matrix():
    s = jnp.arange(KEY_TILE)[:, None]
    j = jnp.arange(KEY_TILE)[None, :]
    half = jnp.concatenate([(s > j).astype(BF16), jnp.ones((KEY_TILE, KEY_TILE), BF16)], axis=1)
    return jnp.concatenate([half, half], axis=0)


def _sb_tile(z, mask, suffix, carry):
    log_beta = jnp.minimum(z, 0.0) - jnp.log(1.0 + jnp.exp(-jnp.abs(z)))
    log_rest = log_beta - z
    if mask is not None:
        log_rest = jnp.where(mask, log_rest, 0.0)
    hi = log_rest.astype(BF16)
    lo = (log_rest - hi.astype(F32)).astype(BF16)
    sums = jnp.dot(jnp.concatenate([hi, lo], axis=1), suffix, preferred_element_type=F32)
    a = jnp.exp(log_beta + carry + sums[:, :KEY_TILE])
    if mask is not None:
        a = jnp.where(mask, a, 0.0)
    return a, carry + sums[:, KEY_TILE:]


def _sb_prompt_body(bias_ref, q_ref, k_ref, v_ref, suf_ref, o_ref, qb_ref, kb_ref, vb_ref, carry_ref,
                    *, n_meta, q_rows, n_chunks, unroll, scale):
    bias = bias_ref[pl.program_id(1)]
    qb_ref[...] = q_ref[...].astype(BF16)
    kb_ref[...] = k_ref[...].astype(BF16)
    vb_ref[...] = v_ref[...].astype(BF16)
    suffix = suf_ref[...]
    sub = q_rows // KEY_TILE

    def weights(qb, ks, mask, carry):
        kt = kb_ref[pl.ds(ks, KEY_TILE), :]
        s = lax.dot_general(qb, kt, (((1,), (1,)), ((), ())), preferred_element_type=F32)
        return _sb_tile(s * scale + bias, mask, suffix, carry)

    def q_chunk(c, _):
        qs = pl.multiple_of(n_meta + c * q_rows, 16)

        def tile(r0, ks, mask):
            rows = q_rows - r0
            a, carry = weights(qb_ref[pl.ds(qs + r0, rows), :], ks, mask, carry_ref[pl.ds(r0, rows), :])
            carry_ref[pl.ds(r0, rows), :] = carry
            o_ref[pl.ds(qs + r0, rows), :] += jnp.dot(a.astype(BF16), vb_ref[pl.ds(ks, KEY_TILE), :],
                                                      preferred_element_type=F32)

        carry_ref[...] = jnp.zeros_like(carry_ref)
        o_ref[pl.ds(qs, q_rows), :] = jnp.zeros((q_rows, KEY_TILE), F32)
        for d in reversed(range(sub)):
            r0 = d * KEY_TILE
            row = lax.broadcasted_iota(jnp.int32, (q_rows - r0, KEY_TILE), 0)
            col = lax.broadcasted_iota(jnp.int32, (q_rows - r0, KEY_TILE), 1)
            tile(r0, pl.multiple_of(qs + r0, 16), col < row)

        def k_step(i, _):
            for u in range(unroll):
                j = c * sub - 1 - (i * unroll + u)
                tile(0, pl.multiple_of(n_meta + j * KEY_TILE, 16), None)
            return 0

        lax.fori_loop(0, c * sub // unroll, k_step, 0)
        col = lax.broadcasted_iota(jnp.int32, (q_rows, KEY_TILE), 1)
        tile(0, 0, col < n_meta)
        return 0

    lax.fori_loop(0, n_chunks, q_chunk, 0)
    row = lax.broadcasted_iota(jnp.int32, (KEY_TILE, KEY_TILE), 0)
    col = lax.broadcasted_iota(jnp.int32, (KEY_TILE, KEY_TILE), 1)
    a, _ = weights(qb_ref[pl.ds(0, KEY_TILE), :], 0, (row > col) & (col < n_meta),
                   jnp.zeros((KEY_TILE, KEY_TILE), F32))
    acc = jnp.dot(a.astype(BF16), vb_ref[pl.ds(0, KEY_TILE), :], preferred_element_type=F32)
    o_ref[pl.ds(0, n_meta), :] = acc[:n_meta]


def _sb_prompt(p, bias, suffix, layer, bsz, t, n_meta, n_heads, head_dim):
    m, d_att = p.shape[0], n_heads * head_dim
    seq = t - n_meta
    assert head_dim == LANES and n_meta % 16 == 0 and seq % KEY_TILE == 0 and t >= KEY_TILE
    q_rows = max(r for r in (4 * KEY_TILE, 2 * KEY_TILE, KEY_TILE) if seq % r == 0)
    unroll = max(u for u in (4, 2, 1) if (q_rows // KEY_TILE) % u == 0)
    body = functools.partial(_sb_prompt_body, n_meta=n_meta, q_rows=q_rows, n_chunks=seq // q_rows,
                             unroll=unroll, scale=head_dim ** -0.5)
    blk = lambda part: pl.BlockSpec((t, head_dim), lambda b, h: (b, part * n_heads + h))
    return pl.pallas_call(
        body,
        grid=(bsz, n_heads),
        in_specs=[
            pl.BlockSpec(memory_space=pltpu.SMEM),
            blk(0), blk(1), blk(2),
            pl.BlockSpec((2 * KEY_TILE, 2 * KEY_TILE), lambda b, h: (0, 0)),
        ],
        out_specs=blk(0),
        out_shape=jax.ShapeDtypeStruct((m, d_att), F32),
        scratch_shapes=[pltpu.VMEM((t, head_dim), BF16)] * 3 + [pltpu.VMEM((q_rows, KEY_TILE), F32)],
        compiler_params=_params("parallel", "parallel"),
        name="sb_prompt",
    )(bias[layer], p, p, p, suffix)


def _sb_sample_body(pt_ref, bias_ref, q_ref, kn_ref, vn_ref, *rest, n_heads, group, scale):
    page_refs, (suf_ref, o_ref, carry_ref) = rest[:2 * group], rest[2 * group:]
    j = pl.program_id(1)
    rows = n_heads * SAMPLE_ROWS
    suffix = suf_ref[...]

    def process(k_ref, v_ref, mask, carry):
        zs = []
        for h in range(n_heads):
            kh = k_ref[pl.ds(h, KEY_TILE, stride=n_heads), :].astype(BF16)
            s = lax.dot_general(q_ref[h], kh, (((1,), (1,)), ((), ())), preferred_element_type=F32)
            zs.append(s * scale + bias_ref[h])
        a, carry = _sb_tile(jnp.concatenate(zs, axis=0), mask, suffix, carry)
        a = a.astype(BF16)
        outs = []
        for h in range(n_heads):
            vh = v_ref[pl.ds(h, KEY_TILE, stride=n_heads), :].astype(BF16)
            outs.append(jnp.dot(a[h * SAMPLE_ROWS:(h + 1) * SAMPLE_ROWS], vh, preferred_element_type=F32))
        return jnp.concatenate(outs, axis=0), carry

    @pl.when(j == 0)
    def _():
        row = lax.broadcasted_iota(jnp.int32, (rows, KEY_TILE), 0) % SAMPLE_ROWS
        col = lax.broadcasted_iota(jnp.int32, (rows, KEY_TILE), 1)
        out, carry = process(kn_ref, vn_ref, col < row, jnp.zeros((rows, KEY_TILE), F32))
        o_ref[...] = out
        carry_ref[...] = carry

    @pl.when(j > 0)
    def _():
        carry = carry_ref[...]
        acc = o_ref[...]
        for g in range(group):
            out, carry = process(page_refs[g], page_refs[group + g], None, carry)
            acc = acc + out
        o_ref[...] = acc
        carry_ref[...] = carry


def _sb_sample(q, k_new, v_new, cache_k, cache_v, page_table, bias, suffix, layer, head_dim):
    bd, n_heads = q.shape[:2]
    n_pages = page_table.shape[1]
    assert cache_k.shape[2] == KEY_TILE * n_heads and head_dim == LANES
    page_rows = KEY_TILE * n_heads
    group = max(g for g in (8, 4, 2, 1) if n_pages % g == 0)
    body = functools.partial(_sb_sample_body, n_heads=n_heads, group=group, scale=head_dim ** -0.5)

    def page_blk(g):
        def index(b, j, pt):
            return (layer, pt[b, n_pages - 1 - ((jnp.maximum(j, 1) - 1) * group + g)], 0, 0)
        return pl.BlockSpec((None, None, page_rows, head_dim), index)

    new_blk = pl.BlockSpec((None, page_rows, head_dim), lambda b, j, pt: (b, 0, 0))
    pages = [page_blk(g) for g in range(group)]
    return pl.pallas_call(
        body,
        grid_spec=pltpu.PrefetchScalarGridSpec(
            num_scalar_prefetch=1,
            grid=(bd, n_pages // group + 1),
            in_specs=[
                pl.BlockSpec(memory_space=pltpu.SMEM),
                pl.BlockSpec((None, n_heads, SAMPLE_ROWS, head_dim), lambda b, j, pt: (b, 0, 0, 0)),
                new_blk, new_blk, *pages, *pages,
                pl.BlockSpec((2 * KEY_TILE, 2 * KEY_TILE), lambda b, j, pt: (0, 0)),
            ],
            out_specs=pl.BlockSpec((None, n_heads * SAMPLE_ROWS, head_dim), lambda b, j, pt: (b, 0, 0)),
            scratch_shapes=[pltpu.VMEM((n_heads * SAMPLE_ROWS, KEY_TILE), F32)],
        ),
        out_shape=jax.ShapeDtypeStruct((bd, n_heads * SAMPLE_ROWS, head_dim), F32),
        compiler_params=_params("parallel", "arbitrary"),
        name="sb_sample",
    )(page_table, bias[layer], q, k_new, v_new, *([cache_k] * group), *([cache_v] * group), suffix)


def _conv_body(a_ref, g_ref, pre_ref, w_ref, b_ref, lg_ref, lb_ref, o_ref, st_ref, ext_ref, sh_ref,
               *, t, t_valid, width, chunk):
    base = CONV_PAD - (width - 1)
    ext_ref[pl.ds(base, width - 1), :] = pre_ref[...]
    n_chunks = t // chunk

    def glu(i, _):
        r = pl.ds(pl.multiple_of(i * chunk, 8), chunk)
        u = a_ref[r, :] * jax.nn.sigmoid(g_ref[r, :])
        ext_ref[pl.ds(pl.multiple_of(CONV_PAD + i * chunk, 8), chunk), :] = u
        return 0

    lax.fori_loop(0, n_chunks, glu, 0)
    st_ref[...] = ext_ref[pl.ds(CONV_PAD + t_valid - (width - 1), width - 1), :]

    def conv(i, _):
        t0 = pl.multiple_of(i * chunk, 8)
        win = ext_ref[pl.ds(t0, chunk + CONV_PAD), :]
        for r in range(1, 8):
            sh_ref[r] = win[r:r + chunk + CONV_PAD - 8]
        acc = jnp.zeros((chunk, a_ref.shape[-1]), F32) + b_ref[...]
        for w in range(width):
            q, r = divmod(base + w, 8)
            if r == 0:
                tap = ext_ref[pl.ds(pl.multiple_of(t0 + 8 * q, 8), chunk), :]
            else:
                tap = sh_ref[r, pl.ds(8 * q, chunk), :]
            acc = acc + tap * w_ref[pl.ds(w, 1), :]
        mean = jnp.mean(acc, axis=-1, keepdims=True)
        cen = acc - mean
        y = cen * lax.rsqrt(jnp.mean(cen * cen, axis=-1, keepdims=True) + EPS) * lg_ref[...] + lb_ref[...]
        o_ref[pl.ds(t0, chunk), :] = (y * jax.nn.sigmoid(y)).astype(o_ref.dtype)
        return 0

    lax.fori_loop(0, n_chunks, conv, 0)


def _conv_branch(p, col_blk, prefix, w, b, ln_g, ln_b, layer, n_seq, t, t_valid):
    width, c = w.shape[1:]
    chunk = _row_tile(t, 48)
    assert t % chunk == 0 and width - 1 <= CONV_PAD and 0 < t_valid <= t
    body = functools.partial(_conv_body, t=t, t_valid=t_valid, width=width, chunk=chunk)
    vec = pl.BlockSpec((None, 1, c), lambda s: (layer, 0, 0))
    return pl.pallas_call(
        body,
        grid=(n_seq,),
        in_specs=[
            pl.BlockSpec((t, c), lambda s: (s, col_blk)),
            pl.BlockSpec((t, c), lambda s: (s, col_blk + 1)),
            pl.BlockSpec((None, width - 1, c), lambda s: (s, 0, 0)),
            pl.BlockSpec((None, width, c), lambda s: (layer, 0, 0)),
            vec, vec, vec,
        ],
        out_specs=[
            pl.BlockSpec((t, c), lambda s: (s, 0)),
            pl.BlockSpec((None, width - 1, c), lambda s: (s, 0, 0)),
        ],
        out_shape=[
            jax.ShapeDtypeStruct((n_seq * t, c), BF16),
            jax.ShapeDtypeStruct((n_seq, width - 1, c), F32),
        ],
        scratch_shapes=[pltpu.VMEM((CONV_PAD + t, c), F32), pltpu.VMEM((8, chunk + CONV_PAD - 8, c), F32)],
        compiler_params=_params("parallel"),
        name="conv_branch",
    )(p, p, prefix, w, b, ln_g, ln_b)


def _lru_body(x_ref, gate_ref, pre_ref, h0_ref, cw_ref, cb_ref, wa_ref, ba_ref, wx_ref, bx_ref, lam_ref, og_ref,
              o_ref, st_ref, hl_ref, ext_ref, a_ref, u_ref, *, t, t_valid, width, chunk):
    base = LRU_PAD - (width - 1)
    ext_ref[pl.ds(base, width - 1), :] = pre_ref[...]
    n_chunks = t // chunk

    def fill(i, _):
        t0 = pl.multiple_of(i * chunk, 8)
        ext_ref[pl.ds(LRU_PAD + t0, chunk), :] = x_ref[pl.ds(t0, chunk), :]
        return 0

    lax.fori_loop(0, n_chunks, fill, 0)
    st_ref[...] = ext_ref[pl.ds(LRU_PAD + t_valid - (width - 1), width - 1), :]
    neg_c_sp = -LRU_C * _softplus(-lam_ref[...])

    def gates(i, _):
        t0 = pl.multiple_of(i * chunk, 8)
        win = ext_ref[pl.ds(t0, chunk + LRU_PAD), :]
        xc = jnp.zeros((chunk, x_ref.shape[-1]), F32) + cb_ref[...]
        for w in range(width):
            xc = xc + win[base + w:base + w + chunk] * cw_ref[pl.ds(w, 1), :]
        xb = xc.astype(BF16)
        r = jax.nn.sigmoid(jnp.dot(xb, wa_ref[...], preferred_element_type=F32) + ba_ref[...])
        ig = jax.nn.sigmoid(jnp.dot(xb, wx_ref[...], preferred_element_type=F32) + bx_ref[...])
        log_a = r * neg_c_sp
        a = jnp.exp(log_a)
        a_ref[pl.ds(t0, chunk), :] = a
        u_ref[pl.ds(t0, chunk), :] = jnp.sqrt(-jnp.tanh(log_a) * (1.0 + a * a)) * (ig * xc)
        return 0

    lax.fori_loop(0, n_chunks, gates, 0)

    def step(s, h):
        h = a_ref[pl.ds(s, 1), :] * h + u_ref[pl.ds(s, 1), :]
        u_ref[pl.ds(s, 1), :] = h
        return h

    hl_ref[...] = lax.fori_loop(0, t_valid, step, h0_ref[...], unroll=min(8, t_valid))

    def out(i, _):
        r = pl.ds(pl.multiple_of(i * chunk, 8), chunk)
        y = u_ref[r, :] * jax.nn.gelu(gate_ref[r, :])
        o_ref[r, :] = _rms(y, og_ref[...]).astype(o_ref.dtype)
        return 0

    lax.fori_loop(0, n_chunks, out, 0)


def _lru_branch(p, col_blk, prefix, h0, cw, cb, wa, ba, wx, bx, lam, out_g, layer, n_seq, t, t_valid, chunk):
    width, c = cw.shape[1:]
    assert t % chunk == 0 and chunk % 8 == 0 and width - 1 <= LRU_PAD
    body = functools.partial(_lru_body, t=t, t_valid=t_valid, width=width, chunk=chunk)
    vec = pl.BlockSpec((None, 1, c), lambda s: (layer, 0, 0))
    mat = pl.BlockSpec((None, c, c), lambda s: (layer, 0, 0))
    return pl.pallas_call(
        body,
        grid=(n_seq,),
        in_specs=[
            pl.BlockSpec((t, c), lambda s: (s, col_blk)),
            pl.BlockSpec((t, c), lambda s: (s, col_blk + 1)),
            pl.BlockSpec((None, width - 1, c), lambda s: (s, 0, 0)),
            pl.BlockSpec((None, 1, c), lambda s: (s, 0, 0)),
            pl.BlockSpec((None, width, c), lambda s: (layer, 0, 0)),
            vec, mat, vec, mat, vec, vec, vec,
        ],
        out_specs=[
            pl.BlockSpec((t, c), lambda s: (s, 0)),
            pl.BlockSpec((None, width - 1, c), lambda s: (s, 0, 0)),
            pl.BlockSpec((None, 1, c), lambda s: (s, 0, 0)),
        ],
        out_shape=[
            jax.ShapeDtypeStruct((n_seq * t, c), BF16),
            jax.ShapeDtypeStruct((n_seq, width - 1, c), F32),
            jax.ShapeDtypeStruct((n_seq, 1, c), F32),
        ],
        scratch_shapes=[pltpu.VMEM((LRU_PAD + t, c), F32), pltpu.VMEM((t, c), F32), pltpu.VMEM((t, c), F32)],
        compiler_params=_params("parallel"),
        name="lru_branch",
    )(p, p, prefix, h0, cw, cb, wa, ba, wx, bx, lam, out_g)


def _block_diag(w):
    depth, nb, bi, bj = w.shape
    eye = jnp.eye(nb, dtype=w.dtype)
    return jnp.einsum("lnij,nm->lnimj", w, eye).reshape(depth, nb * bi, nb * bj)


def kernel(x_prompt, x_sample, cache_k, cache_v, state_conv, state_lru_conv, state_lru_h, page_table, meta_tokens, norm_mix, w_in, sb_bias, conv_w, conv_b, conv_ln_g, conv_ln_b, lru_conv_w, lru_conv_b, lru_wa, lru_ba, lru_wx, lru_bx, lru_lambda, attn_out_g, lru_out_g, w_out, norm_ffn, w_gate, w_up, w_down, final_norm):
    bp, seq, d_model = x_prompt.shape
    bs, dec_seq, _ = x_sample.shape
    depth, n_pool, page, n_heads, head_dim = cache_k.shape
    n_meta = meta_tokens.shape[0]
    d_att = n_heads * head_dim
    d_conv = conv_w.shape[-1]
    d_lru = lru_conv_w.shape[-1]
    conv_width = conv_w.shape[1]
    lru_width = lru_conv_w.shape[1]
    t_p = n_meta + seq
    t_s = SAMPLE_ROWS
    assert dec_seq <= SAMPLE_ROWS and d_conv == d_lru and d_att % d_conv == 0
    assert w_in.shape[-1] == 3 * d_att + 2 * d_conv + 2 * d_lru

    w_in_b, w_out_b = w_in.astype(BF16), w_out.astype(BF16)
    w_gate_b, w_up_b, w_down_b = w_gate.astype(BF16), w_up.astype(BF16), w_down.astype(BF16)
    wa_b, wx_b = _block_diag(lru_wa).astype(BF16), _block_diag(lru_wx).astype(BF16)
    row = lambda v: v.reshape(depth, 1, -1)
    norm_mix_r, norm_ffn_r, attn_g_r, lru_g_r = row(norm_mix), row(norm_ffn), row(attn_out_g), row(lru_out_g)
    conv_b_r, ln_g_r, ln_b_r = row(conv_b), row(conv_ln_g), row(conv_ln_b)
    lcb_r, ba_r, bx_r, lam_r = row(lru_conv_b), row(lru_ba), row(lru_bx), row(lru_lambda)
    suffix = _suffix_matrix()
    cache_k2 = cache_k.reshape(depth, n_pool, page * n_heads, head_dim)
    cache_v2 = cache_v.reshape(depth, n_pool, page * n_heads, head_dim)

    meta = jnp.broadcast_to(meta_tokens[None].astype(x_prompt.dtype), (bp, n_meta, d_model))
    xp = jnp.concatenate([meta, x_prompt], axis=1).reshape(bp * t_p, d_model)
    xs = jnp.pad(x_sample, ((0, 0), (0, t_s - dec_seq), (0, 0))).reshape(bs * t_s, d_model)

    zero_conv = jnp.zeros((bp, conv_width - 1, d_conv), F32)
    zero_lru = jnp.zeros((bp, lru_width - 1, d_lru), F32)
    zero_h = jnp.zeros((bp, 1, d_lru), F32)
    conv_blk = 3 * d_att // d_conv
    lru_chunk_p = _row_tile(t_p, 688)

    d_in = w_in.shape[-1]

    def mixers(x, layer, n_seq, t, t_valid, conv_pre, lru_pre, h0, lru_chunk):
        p = _norm_matmul(x, norm_mix_r, w_in_b, layer, 0, d_in, _col_tile(d_in, 1024), F32)
        o_conv, st_conv = _conv_branch(p, conv_blk, conv_pre, conv_w, conv_b_r, ln_g_r, ln_b_r, layer, n_seq, t, t_valid)
        o_lru, st_lru, h_last = _lru_branch(p, conv_blk + 2, lru_pre, h0, lru_conv_w, lcb_r, wa_b, ba_r, wx_b, bx_r,
                                            lam_r, lru_g_r, layer, n_seq, t, t_valid, lru_chunk)
        return p, p[:, d_att:2 * d_att], p[:, 2 * d_att:3 * d_att], o_conv, o_lru, st_conv, st_lru, h_last

    def finish(x, layer, att, o_conv, o_lru):
        x = _mix_out(x, att, attn_g_r, o_conv, o_lru, w_out_b, layer, _col_tile(d_model, 1024))
        act = _norm_gateup(x, norm_ffn_r, w_gate_b, w_up_b, layer, 512)
        return _res_matmul(x, act, w_down_b, layer, 512)

    prompt_states, sample_states = [], []
    for layer in range(depth):
        p, k, v, o_conv, o_lru, st_conv, st_lru, h_last = mixers(
            xp, layer, bp, t_p, t_p, zero_conv, zero_lru, zero_h, lru_chunk_p)
        att = _sb_prompt(p, sb_bias, suffix, layer, bp, t_p, n_meta, n_heads, head_dim)
        xp = finish(xp, layer, att, o_conv, o_lru)
        prompt_states.append((k.reshape(bp, t_p, n_heads, head_dim), v.reshape(bp, t_p, n_heads, head_dim),
                              st_conv, st_lru, h_last.reshape(bp, d_lru)))

        p, k, v, o_conv, o_lru, st_conv, st_lru, h_last = mixers(
            xs, layer, bs, t_s, dec_seq, state_conv[layer], state_lru_conv[layer],
            state_lru_h[layer].reshape(bs, 1, d_lru), t_s)
        q4 = p[:, :d_att].astype(BF16).reshape(bs, t_s, n_heads, head_dim).transpose(0, 2, 1, 3)
        pad_new = lambda a: jnp.pad(a.reshape(bs, t_s, d_att)[:, :dec_seq], ((0, 0), (0, KEY_TILE - dec_seq), (0, 0))
                                    ).reshape(bs, KEY_TILE * n_heads, head_dim)
        att = _sb_sample(q4, pad_new(k), pad_new(v), cache_k2, cache_v2, page_table, sb_bias, suffix, layer, head_dim)
        att = att.reshape(bs, n_heads, t_s, head_dim).transpose(0, 2, 1, 3).reshape(bs * t_s, d_att)
        xs = finish(xs, layer, att, o_conv, o_lru)
        new = lambda a: a.reshape(bs, t_s, n_heads, head_dim)[:, :dec_seq]
        sample_states.append((new(k), new(v), st_conv, st_lru, h_last.reshape(bs, d_lru)))

    stack = lambda states: [jnp.stack([s[i] for s in states], axis=0) for i in range(5)]
    kp, vp, cp, lcp, hp = stack(prompt_states)
    ksm, vsm, csm, lcsm, hsm = stack(sample_states)
    fin = final_norm.reshape(1, d_model)
    y_prompt = _final_norm(xp.reshape(bp, t_p, d_model)[:, n_meta:].reshape(bp * seq, d_model), fin)
    y_sample = _final_norm(xs, fin).reshape(bs, t_s, d_model)[:, :dec_seq]
    return (y_prompt.reshape(bp, seq, d_model), y_sample, kp, vp, cp, lcp, hp, ksm, vsm, csm, lcsm, hsm)
```

```python
import functools

import jax
import jax.numpy as jnp
from jax import lax
from jax.experimental import pallas as pl
from jax.experimental.pallas import tpu as pltpu

F32 = jnp.float32
BF16 = jnp.bfloat16

EPS = 1e-6
LRU_C = 8.0
LANES = 128
KEY_TILE = 128
SAMPLE_ROWS = 16
CONV_PAD = 32
LRU_PAD = 8
VMEM_LIMIT_BYTES = 56 * 1024 * 1024


def _params(*sem):
    return pltpu.CompilerParams(dimension_semantics=sem, vmem_limit_bytes=VMEM_LIMIT_BYTES)


def _row_tile(m, cap):
    if m <= cap:
        return m
    best = None
    for t in range(16, cap + 1, 16):
        if m % t == 0:
            best = t
    assert best is not None, m
    return best


def _col_tile(n, cap):
    best = max(t for t in range(LANES, min(n, cap) + 1, LANES) if n % t == 0)
    return best


def _rms(x, g):
    return x * lax.rsqrt(jnp.mean(x * x, axis=-1, keepdims=True) + EPS) * g


def _softplus(x):
    return jnp.maximum(x, 0.0) + jnp.log1p(jnp.exp(-jnp.abs(x)))


def _norm_mm_body(x_ref, g_ref, w_ref, o_ref, u_ref):
    @pl.when(pl.program_id(1) == 0)
    def _():
        u_ref[...] = _rms(x_ref[...], g_ref[...]).astype(BF16)

    o_ref[...] = jnp.dot(u_ref[...], w_ref[...], preferred_element_type=F32).astype(o_ref.dtype)


def _norm_matmul(x, g, w, layer, col0, ncols, tn, out_dtype):
    m, k = x.shape
    tm = _row_tile(m, 688)
    off = col0 // tn
    assert col0 % tn == 0 and ncols % tn == 0
    return pl.pallas_call(
        _norm_mm_body,
        grid=(m // tm, ncols // tn),
        in_specs=[
            pl.BlockSpec((tm, k), lambda i, j: (i, 0)),
            pl.BlockSpec((None, 1, k), lambda i, j: (layer, 0, 0)),
            pl.BlockSpec((None, k, tn), lambda i, j: (layer, 0, j + off)),
        ],
        out_specs=pl.BlockSpec((tm, tn), lambda i, j: (i, j)),
        out_shape=jax.ShapeDtypeStruct((m, ncols), out_dtype),
        scratch_shapes=[pltpu.VMEM((tm, k), BF16)],
        compiler_params=_params("parallel", "arbitrary"),
        name="in_proj",
    )(x, g, w)


def _norm_gateup_body(x_ref, g_ref, wg_ref, wu_ref, o_ref, u_ref):
    @pl.when(pl.program_id(1) == 0)
    def _():
        u_ref[...] = _rms(x_ref[...], g_ref[...]).astype(BF16)

    u = u_ref[...]
    gate = jnp.dot(u, wg_ref[...], preferred_element_type=F32)
    up = jnp.dot(u, wu_ref[...], preferred_element_type=F32)
    o_ref[...] = (gate * jax.nn.sigmoid(gate) * up).astype(o_ref.dtype)


def _norm_gateup(x, g, wg, wu, layer, tn):
    m, k = x.shape
    n = wg.shape[-1]
    tm = _row_tile(m, 688)
    return pl.pallas_call(
        _norm_gateup_body,
        grid=(m // tm, n // tn),
        in_specs=[
            pl.BlockSpec((tm, k), lambda i, j: (i, 0)),
            pl.BlockSpec((None, 1, k), lambda i, j: (layer, 0, 0)),
            pl.BlockSpec((None, k, tn), lambda i, j: (layer, 0, j)),
            pl.BlockSpec((None, k, tn), lambda i, j: (layer, 0, j)),
        ],
        out_specs=pl.BlockSpec((tm, tn), lambda i, j: (i, j)),
        out_shape=jax.ShapeDtypeStruct((m, n), BF16),
        scratch_shapes=[pltpu.VMEM((tm, k), BF16)],
        compiler_params=_params("parallel", "arbitrary"),
        name="ffn_gate_up",
    )(x, g, wg, wu)


def _res_mm_body(r_ref, a_ref, w_ref, o_ref):
    o_ref[...] = r_ref[...] + jnp.dot(a_ref[...], w_ref[...], preferred_element_type=F32)


def _res_matmul(res, a, w, layer, tn):
    m, k = a.shape
    n = w.shape[-1]
    tm = _row_tile(m, 688)
    return pl.pallas_call(
        _res_mm_body,
        grid=(m // tm, n // tn),
        in_specs=[
            pl.BlockSpec((tm, tn), lambda i, j: (i, j)),
            pl.BlockSpec((tm, k), lambda i, j: (i, 0)),
            pl.BlockSpec((None, k, tn), lambda i, j: (layer, 0, j)),
        ],
        out_specs=pl.BlockSpec((tm, tn), lambda i, j: (i, j)),
        out_shape=jax.ShapeDtypeStruct((m, n), F32),
        compiler_params=_params("parallel", "arbitrary"),
        name="ffn_down",
    )(res, a, w)


def _mix_out_body(r_ref, att_ref, ag_ref, conv_ref, lru_ref, wa_ref, wc_ref, wl_ref, o_ref, an_ref):
    @pl.when(pl.program_id(1) == 0)
    def _():
        an_ref[...] = _rms(att_ref[...], ag_ref[...]).astype(BF16)

    acc = jnp.dot(an_ref[...], wa_ref[...], preferred_element_type=F32)
    acc += jnp.dot(conv_ref[...], wc_ref[...], preferred_element_type=F32)
    acc += jnp.dot(lru_ref[...], wl_ref[...], preferred_element_type=F32)
    o_ref[...] = r_ref[...] + acc


def _mix_out(res, att, att_g, conv, lru, w, layer, tn):
    m, n = res.shape
    d_att, d_conv, d_lru = att.shape[1], conv.shape[1], lru.shape[1]
    tm = _row_tile(m, 688)
    assert d_att % d_conv == 0 and d_conv == d_lru
    c_blk = d_att // d_conv
    return pl.pallas_call(
        _mix_out_body,
        grid=(m // tm, n // tn),
        in_specs=[
            pl.BlockSpec((tm, tn), lambda i, j: (i, j)),
            pl.BlockSpec((tm, d_att), lambda i, j: (i, 0)),
            pl.BlockSpec((None, 1, d_att), lambda i, j: (layer, 0, 0)),
            pl.BlockSpec((tm, d_conv), lambda i, j: (i, 0)),
            pl.BlockSpec((tm, d_lru), lambda i, j: (i, 0)),
            pl.BlockSpec((None, d_att, tn), lambda i, j: (layer, 0, j)),
            pl.BlockSpec((None, d_conv, tn), lambda i, j: (layer, c_blk, j)),
            pl.BlockSpec((None, d_lru, tn), lambda i, j: (layer, c_blk + 1, j)),
        ],
        out_specs=pl.BlockSpec((tm, tn), lambda i, j: (i, j)),
        out_shape=jax.ShapeDtypeStruct((m, n), F32),
        scratch_shapes=[pltpu.VMEM((tm, d_att), BF16)],
        compiler_params=_params("parallel", "arbitrary"),
        name="mix_out",
    )(res, att, att_g, conv, lru, w, w, w)


def _final_norm_body(x_ref, g_ref, o_ref):
    o_ref[...] = _rms(x_ref[...], g_ref[...])


def _final_norm(x, g):
    m, d = x.shape
    tm = _row_tile(m, 512)
    return pl.pallas_call(
        _final_norm_body,
        grid=(m // tm,),
        in_specs=[pl.BlockSpec((tm, d), lambda i: (i, 0)), pl.BlockSpec((1, d), lambda i: (0, 0))],
        out_specs=pl.BlockSpec((tm, d), lambda i: (i, 0)),
        out_shape=jax.ShapeDtypeStruct((m, d), F32),
        compiler_params=_params("parallel"),
        name="final_norm",
    )(x, g)


def _suffix_matrix():
    s = jnp.arange(KEY_TILE)[:, None]
    j = jnp.arange(KEY_TILE)[None, :]
    half = jnp.concatenate([(s > j).astype(BF16), jnp.ones((KEY_TILE, KEY_TILE), BF16)], axis=1)
    return jnp.concatenate([half, half], axis=0)


def _sb_group(z, mask, suffix, carry):
    n_tiles = z.shape[1] // KEY_TILE
    log_beta = jnp.minimum(z, 0.0) - jnp.log(1.0 + jnp.exp(-jnp.abs(z)))
    log_rest = log_beta - z
    if mask is not None:
        log_rest = jnp.where(mask, log_rest, 0.0)
    hi = log_rest.astype(BF16)
    lo = (log_rest - hi.astype(F32)).astype(BF16)
    sums = []
    for t in range(n_tiles):
        cols = slice(t * KEY_TILE, (t + 1) * KEY_TILE)
        sums.append(jnp.dot(jnp.concatenate([hi[:, cols], lo[:, cols]], axis=1), suffix,
                            preferred_element_type=F32))
    later = [None] * n_tiles
    for t in reversed(range(n_tiles)):
        later[t] = carry + sums[t][:, :KEY_TILE]
        carry = carry + sums[t][:, KEY_TILE:]
    a = jnp.exp(log_beta + jnp.concatenate(later, axis=1))
    if mask is not None:
        a = jnp.where(mask, a, 0.0)
    return a, carry


def _sb_prompt_body(bias_ref, q_ref, k_ref, v_ref, suf_ref, o_ref, qb_ref, kb_ref, vb_ref, carry_ref,
                    *, n_meta, q_rows, n_chunks, scale):
    bias = bias_ref[pl.program_id(1)]
    qb_ref[...] = q_ref[...].astype(BF16)
    kb_ref[...] = k_ref[...].astype(BF16)
    vb_ref[...] = v_ref[...].astype(BF16)
    suffix = suf_ref[...]
    half = q_rows // 2

    def weights(qb, ks, n_keys, mask, carry):
        kt = kb_ref[pl.ds(ks, n_keys), :]
        s = lax.dot_general(qb, kt, (((1,), (1,)), ((), ())), preferred_element_type=F32)
        return _sb_group(s * scale + bias, mask, suffix, carry)

    def q_chunk(c, _):
        qs = pl.multiple_of(n_meta + c * q_rows, 16)

        def group(r0, rows, ks, n_keys, mask):
            a, carry = weights(qb_ref[pl.ds(qs + r0, rows), :], ks, n_keys, mask, carry_ref[pl.ds(r0, rows), :])
            carry_ref[pl.ds(r0, rows), :] = carry
            o_ref[pl.ds(qs + r0, rows), :] += jnp.dot(a.astype(BF16), vb_ref[pl.ds(ks, n_keys), :],
                                                      preferred_element_type=F32)

        carry_ref[...] = jnp.zeros_like(carry_ref)
        o_ref[pl.ds(qs, q_rows), :] = jnp.zeros((q_rows, KEY_TILE), F32)
        for r0, n_keys in ((half, q_rows), (0, half)):
            row = lax.broadcasted_iota(jnp.int32, (half, n_keys), 0)
            col = lax.broadcasted_iota(jnp.int32, (half, n_keys), 1)
            group(r0, half, qs, n_keys, col < row + r0)

        def k_step(i, _):
            group(0, q_rows, pl.multiple_of(n_meta + (c - 1 - i) * q_rows, 16), q_rows, None)
            return 0

        lax.fori_loop(0, c, k_step, 0)
        col = lax.broadcasted_iota(jnp.int32, (q_rows, KEY_TILE), 1)
        group(0, q_rows, 0, KEY_TILE, col < n_meta)
        return 0

    lax.fori_loop(0, n_chunks, q_chunk, 0)
    row = lax.broadcasted_iota(jnp.int32, (KEY_TILE, KEY_TILE), 0)
    col = lax.broadcasted_iota(jnp.int32, (KEY_TILE, KEY_TILE), 1)
    a, _ = weights(qb_ref[pl.ds(0, KEY_TILE), :], 0, KEY_TILE, (row > col) & (col < n_meta),
                   jnp.zeros((KEY_TILE, KEY_TILE), F32))
    acc = jnp.dot(a.astype(BF16), vb_ref[pl.ds(0, KEY_TILE), :], preferred_element_type=F32)
    o_ref[pl.ds(0, n_meta), :] = acc[:n_meta]


def _sb_prompt(p, bias, suffix, layer, bsz, t, n_meta, n_heads, head_dim):
    m, d_att = p.shape[0], n_heads * head_dim
    seq = t - n_meta
    assert head_dim == LANES and n_meta % 16 == 0 and seq % (2 * KEY_TILE) == 0
    q_rows = 4 * KEY_TILE if seq % (4 * KEY_TILE) == 0 else 2 * KEY_TILE
    body = functools.partial(_sb_prompt_body, n_meta=n_meta, q_rows=q_rows, n_chunks=seq // q_rows,
                             scale=head_dim ** -0.5)
    blk = lambda part: pl.BlockSpec((t, head_dim), lambda b, h: (b, part * n_heads + h))
    return pl.pallas_call(
        body,
        grid=(bsz, n_heads),
        in_specs=[
            pl.BlockSpec(memory_space=pltpu.SMEM),
            blk(0), blk(1), blk(2),
            pl.BlockSpec((2 * KEY_TILE, 2 * KEY_TILE), lambda b, h: (0, 0)),
        ],
        out_specs=blk(0),
        out_shape=jax.ShapeDtypeStruct((m, d_att), F32),
        scratch_shapes=[pltpu.VMEM((t, head_dim), BF16)] * 3 + [pltpu.VMEM((q_rows, KEY_TILE), F32)],
        compiler_params=_params("parallel", "parallel"),
        name="sb_prompt",
    )(bias[layer], p, p, p, suffix)


def _sb_sample_body(pt_ref, bias_ref, q_ref, kn_ref, vn_ref, *rest, n_heads, group, scale):
    page_refs, (suf_ref, o_ref, carry_ref) = rest[:2 * group], rest[2 * group:]
    j = pl.program_id(1)
    rows = n_heads * SAMPLE_ROWS
    suffix = suf_ref[...]

    def process(k_refs, v_refs, mask, carry):
        zs = []
        for k_ref in k_refs:
            zt = []
            for h in range(n_heads):
                kh = k_ref[pl.ds(h, KEY_TILE, stride=n_heads), :].astype(BF16)
                s = lax.dot_general(q_ref[h], kh, (((1,), (1,)), ((), ())), preferred_element_type=F32)
                zt.append(s * scale + bias_ref[h])
            zs.append(jnp.concatenate(zt, axis=0))
        a, carry = _sb_group(jnp.concatenate(zs, axis=1), mask, suffix, carry)
        a = a.astype(BF16)
        outs = []
        for h in range(n_heads):
            rows_h = slice(h * SAMPLE_ROWS, (h + 1) * SAMPLE_ROWS)
            acc = None
            for t, v_ref in enumerate(v_refs):
                vh = v_ref[pl.ds(h, KEY_TILE, stride=n_heads), :].astype(BF16)
                part = jnp.dot(a[rows_h, t * KEY_TILE:(t + 1) * KEY_TILE], vh, preferred_element_type=F32)
                acc = part if acc is None else acc + part
            outs.append(acc)
        return jnp.concatenate(outs, axis=0), carry

    @pl.when(j == 0)
    def _():
        row = lax.broadcasted_iota(jnp.int32, (rows, KEY_TILE), 0) % SAMPLE_ROWS
        col = lax.broadcasted_iota(jnp.int32, (rows, KEY_TILE), 1)
        out, carry = process([kn_ref], [vn_ref], col < row, jnp.zeros((rows, KEY_TILE), F32))
        o_ref[...] = out
        carry_ref[...] = carry

    @pl.when(j > 0)
    def _():
        out, carry = process(page_refs[:group][::-1], page_refs[group:][::-1], None, carry_ref[...])
        o_ref[...] += out
        carry_ref[...] = carry


def _sb_sample(q, k_new, v_new, cache_k, cache_v, page_table, bias, suffix, layer, head_dim):
    bd, n_heads = q.shape[:2]
    n_pages = page_table.shape[1]
    assert cache_k.shape[2] == KEY_TILE * n_heads and head_dim == LANES
    page_rows = KEY_TILE * n_heads
    group = max(g for g in (8, 4, 2, 1) if n_pages % g == 0)
    body = functools.partial(_sb_sample_body, n_heads=n_heads, group=group, scale=head_dim ** -0.5)

    def page_blk(g):
        def index(b, j, pt):
            return (layer, pt[b, n_pages - 1 - ((jnp.maximum(j, 1) - 1) * group + g)], 0, 0)
        return pl.BlockSpec((None, None, page_rows, head_dim), index)

    new_blk = pl.BlockSpec((None, page_rows, head_dim), lambda b, j, pt: (b, 0, 0))
    pages = [page_blk(g) for g in range(group)]
    return pl.pallas_call(
        body,
        grid_spec=pltpu.PrefetchScalarGridSpec(
            num_scalar_prefetch=1,
            grid=(bd, n_pages // group + 1),
            in_specs=[
                pl.BlockSpec(memory_space=pltpu.SMEM),
                pl.BlockSpec((None, n_heads, SAMPLE_ROWS, head_dim), lambda b, j, pt: (b, 0, 0, 0)),
                new_blk, new_blk, *pages, *pages,
                pl.BlockSpec((2 * KEY_TILE, 2 * KEY_TILE), lambda b, j, pt: (0, 0)),
            ],
            out_specs=pl.BlockSpec((None, n_heads * SAMPLE_ROWS, head_dim), lambda b, j, pt: (b, 0, 0)),
            scratch_shapes=[pltpu.VMEM((n_heads * SAMPLE_ROWS, KEY_TILE), F32)],
        ),
        out_shape=jax.ShapeDtypeStruct((bd, n_heads * SAMPLE_ROWS, head_dim), F32),
        compiler_params=_params("parallel", "arbitrary"),
        name="sb_sample",
    )(page_table, bias[layer], q, k_new, v_new, *([cache_k] * group), *([cache_v] * group), suffix)


def _conv_body(a_ref, g_ref, pre_ref, w_ref, b_ref, lg_ref, lb_ref, o_ref, st_ref, ext_ref, sh_ref,
               *, t, t_valid, width, chunk):
    base = CONV_PAD - (width - 1)
    ext_ref[pl.ds(base, width - 1), :] = pre_ref[...]
    n_chunks = t // chunk

    def glu(i, _):
        r = pl.ds(pl.multiple_of(i * chunk, 8), chunk)
        u = a_ref[r, :] * jax.nn.sigmoid(g_ref[r, :])
        ext_ref[pl.ds(pl.multiple_of(CONV_PAD + i * chunk, 8), chunk), :] = u
        return 0

    lax.fori_loop(0, n_chunks, glu, 0)
    st_ref[...] = ext_ref[pl.ds(CONV_PAD + t_valid - (width - 1), width - 1), :]

    def conv(i, _):
        t0 = pl.multiple_of(i * chunk, 8)
        win = ext_ref[pl.ds(t0, chunk + CONV_PAD), :]
        for r in range(1, 8):
            sh_ref[r] = win[r:r + chunk + CONV_PAD - 8]
        acc = jnp.zeros((chunk, a_ref.shape[-1]), F32) + b_ref[...]
        for w in range(width):
            q, r = divmod(base + w, 8)
            if r == 0:
                tap = ext_ref[pl.ds(pl.multiple_of(t0 + 8 * q, 8), chunk), :]
            else:
                tap = sh_ref[r, pl.ds(8 * q, chunk), :]
            acc = acc + tap * w_ref[pl.ds(w, 1), :]
        mean = jnp.mean(acc, axis=-1, keepdims=True)
        cen = acc - mean
        y = cen * lax.rsqrt(jnp.mean(cen * cen, axis=-1, keepdims=True) + EPS) * lg_ref[...] + lb_ref[...]
        o_ref[pl.ds(t0, chunk), :] = (y * jax.nn.sigmoid(y)).astype(o_ref.dtype)
        return 0

    lax.fori_loop(0, n_chunks, conv, 0)


def _conv_branch(p, col_blk, prefix, w, b, ln_g, ln_b, layer, n_seq, t, t_valid):
    width, c = w.shape[1:]
    chunk = _row_tile(t, 48)
    assert t % chunk == 0 and width - 1 <= CONV_PAD and 0 < t_valid <= t
    body = functools.partial(_conv_body, t=t, t_valid=t_valid, width=width, chunk=chunk)
    vec = pl.BlockSpec((None, 1, c), lambda s: (layer, 0, 0))
    return pl.pallas_call(
        body,
        grid=(n_seq,),
        in_specs=[
            pl.BlockSpec((t, c), lambda s: (s, col_blk)),
            pl.BlockSpec((t, c), lambda s: (s, col_blk + 1)),
            pl.BlockSpec((None, width - 1, c), lambda s: (s, 0, 0)),
            pl.BlockSpec((None, width, c), lambda s: (layer, 0, 0)),
            vec, vec, vec,
        ],
        out_specs=[
            pl.BlockSpec((t, c), lambda s: (s, 0)),
            pl.BlockSpec((None, width - 1, c), lambda s: (s, 0, 0)),
        ],
        out_shape=[
            jax.ShapeDtypeStruct((n_seq * t, c), BF16),
            jax.ShapeDtypeStruct((n_seq, width - 1, c), F32),
        ],
        scratch_shapes=[pltpu.VMEM((CONV_PAD + t, c), F32), pltpu.VMEM((8, chunk + CONV_PAD - 8, c), F32)],
        compiler_params=_params("parallel"),
        name="conv_branch",
    )(p, p, prefix, w, b, ln_g, ln_b)


def _lru_body(x_ref, gate_ref, pre_ref, h0_ref, cw_ref, cb_ref, wa_ref, ba_ref, wx_ref, bx_ref, lam_ref, og_ref,
              o_ref, st_ref, hl_ref, ext_ref, a_ref, u_ref, *, t, t_valid, width, chunk):
    base = LRU_PAD - (width - 1)
    ext_ref[pl.ds(base, width - 1), :] = pre_ref[...]
    n_chunks = t // chunk

    def fill(i, _):
        t0 = pl.multiple_of(i * chunk, 8)
        ext_ref[pl.ds(LRU_PAD + t0, chunk), :] = x_ref[pl.ds(t0, chunk), :]
        return 0

    lax.fori_loop(0, n_chunks, fill, 0)
    st_ref[...] = ext_ref[pl.ds(LRU_PAD + t_valid - (width - 1), width - 1), :]
    neg_c_sp = -LRU_C * _softplus(-lam_ref[...])

    def gates(i, _):
        t0 = pl.multiple_of(i * chunk, 8)
        win = ext_ref[pl.ds(t0, chunk + LRU_PAD), :]
        xc = jnp.zeros((chunk, x_ref.shape[-1]), F32) + cb_ref[...]
        for w in range(width):
            xc = xc + win[base + w:base + w + chunk] * cw_ref[pl.ds(w, 1), :]
        xb = xc.astype(BF16)
        r = jax.nn.sigmoid(jnp.dot(xb, wa_ref[...], preferred_element_type=F32) + ba_ref[...])
        ig = jax.nn.sigmoid(jnp.dot(xb, wx_ref[...], preferred_element_type=F32) + bx_ref[...])
        log_a = r * neg_c_sp
        a = jnp.exp(log_a)
        a_ref[pl.ds(t0, chunk), :] = a
        u_ref[pl.ds(t0, chunk), :] = jnp.sqrt(-jnp.tanh(log_a) * (1.0 + a * a)) * (ig * xc)
        return 0

    lax.fori_loop(0, n_chunks, gates, 0)

    def step(s, h):
        h = a_ref[pl.ds(s, 1), :] * h + u_ref[pl.ds(s, 1), :]
        u_ref[pl.ds(s, 1), :] = h
        return h

    hl_ref[...] = lax.fori_loop(0, t_valid, step, h0_ref[...], unroll=min(8, t_valid))

    def out(i, _):
        r = pl.ds(pl.multiple_of(i * chunk, 8), chunk)
        y = u_ref[r, :] * jax.nn.gelu(gate_ref[r, :])
        o_ref[r, :] = _rms(y, og_ref[...]).astype(o_ref.dtype)
        return 0

    lax.fori_loop(0, n_chunks, out, 0)


def _lru_branch(p, col_blk, prefix, h0, cw, cb, wa, ba, wx, bx, lam, out_g, layer, n_seq, t, t_valid, chunk):
    width, c = cw.shape[1:]
    assert t % chunk == 0 and chunk % 8 == 0 and width - 1 <= LRU_PAD
    body = functools.partial(_lru_body, t=t, t_valid=t_valid, width=width, chunk=chunk)
    vec = pl.BlockSpec((None, 1, c), lambda s: (layer, 0, 0))
    mat = pl.BlockSpec((None, c, c), lambda s: (layer, 0, 0))
    return pl.pallas_call(
        body,
        grid=(n_seq,),
        in_specs=[
            pl.BlockSpec((t, c), lambda s: (s, col_blk)),
            pl.BlockSpec((t, c), lambda s: (s, col_blk + 1)),
            pl.BlockSpec((None, width - 1, c), lambda s: (s, 0, 0)),
            pl.BlockSpec((None, 1, c), lambda s: (s, 0, 0)),
            pl.BlockSpec((None, width, c), lambda s: (layer, 0, 0)),
            vec, mat, vec, mat, vec, vec, vec,
        ],
        out_specs=[
            pl.BlockSpec((t, c), lambda s: (s, 0)),
            pl.BlockSpec((None, width - 1, c), lambda s: (s, 0, 0)),
            pl.BlockSpec((None, 1, c), lambda s: (s, 0, 0)),
        ],
        out_shape=[
            jax.ShapeDtypeStruct((n_seq * t, c), BF16),
            jax.ShapeDtypeStruct((n_seq, width - 1, c), F32),
            jax.ShapeDtypeStruct((n_seq, 1, c), F32),
        ],
        scratch_shapes=[pltpu.VMEM((LRU_PAD + t, c), F32), pltpu.VMEM((t, c), F32), pltpu.VMEM((t, c), F32)],
        compiler_params=_params("parallel"),
        name="lru_branch",
    )(p, p, prefix, h0, cw, cb, wa, ba, wx, bx, lam, out_g)


def _block_diag(w):
    depth, nb, bi, bj = w.shape
    eye = jnp.eye(nb, dtype=w.dtype)
    return jnp.einsum("lnij,nm->lnimj", w, eye).reshape(depth, nb * bi, nb * bj)


def kernel(x_prompt, x_sample, cache_k, cache_v, state_conv, state_lru_conv, state_lru_h, page_table, meta_tokens, norm_mix, w_in, sb_bias, conv_w, conv_b, conv_ln_g, conv_ln_b, lru_conv_w, lru_conv_b, lru_wa, lru_ba, lru_wx, lru_bx, lru_lambda, attn_out_g, lru_out_g, w_out, norm_ffn, w_gate, w_up, w_down, final_norm):
    bp, seq, d_model = x_prompt.shape
    bs, dec_seq, _ = x_sample.shape
    depth, n_pool, page, n_heads, head_dim = cache_k.shape
    n_meta = meta_tokens.shape[0]
    d_att = n_heads * head_dim
    d_conv = conv_w.shape[-1]
    d_lru = lru_conv_w.shape[-1]
    conv_width = conv_w.shape[1]
    lru_width = lru_conv_w.shape[1]
    t_p = n_meta + seq
    t_s = SAMPLE_ROWS
    assert dec_seq <= SAMPLE_ROWS and d_conv == d_lru and d_att % d_conv == 0
    assert w_in.shape[-1] == 3 * d_att + 2 * d_conv + 2 * d_lru

    w_in_b, w_out_b = w_in.astype(BF16), w_out.astype(BF16)
    w_gate_b, w_up_b, w_down_b = w_gate.astype(BF16), w_up.astype(BF16), w_down.astype(BF16)
    wa_b, wx_b = _block_diag(lru_wa).astype(BF16), _block_diag(lru_wx).astype(BF16)
    row = lambda v: v.reshape(depth, 1, -1)
    norm_mix_r, norm_ffn_r, attn_g_r, lru_g_r = row(norm_mix), row(norm_ffn), row(attn_out_g), row(lru_out_g)
    conv_b_r, ln_g_r, ln_b_r = row(conv_b), row(conv_ln_g), row(conv_ln_b)
    lcb_r, ba_r, bx_r, lam_r = row(lru_conv_b), row(lru_ba), row(lru_bx), row(lru_lambda)
    suffix = _suffix_matrix()
    cache_k2 = cache_k.reshape(depth, n_pool, page * n_heads, head_dim)
    cache_v2 = cache_v.reshape(depth, n_pool, page * n_heads, head_dim)

    meta = jnp.broadcast_to(meta_tokens[None].astype(x_prompt.dtype), (bp, n_meta, d_model))
    xp = jnp.concatenate([meta, x_prompt], axis=1).reshape(bp * t_p, d_model)
    xs = jnp.pad(x_sample, ((0, 0), (0, t_s - dec_seq), (0, 0))).reshape(bs * t_s, d_model)

    zero_conv = jnp.zeros((bp, conv_width - 1, d_conv), F32)
    zero_lru = jnp.zeros((bp, lru_width - 1, d_lru), F32)
    zero_h = jnp.zeros((bp, 1, d_lru), F32)
    conv_blk = 3 * d_att // d_conv
    lru_chunk_p = _row_tile(t_p, 688)

    d_in = w_in.shape[-1]

    def mixers(x, layer, n_seq, t, t_valid, conv_pre, lru_pre, h0, lru_chunk):
        p = _norm_matmul(x, norm_mix_r, w_in_b, layer, 0, d_in, _col_tile(d_in, 2048), F32)
        o_conv, st_conv = _conv_branch(p, conv_blk, conv_pre, conv_w, conv_b_r, ln_g_r, ln_b_r, layer, n_seq, t, t_valid)
        o_lru, st_lru, h_last = _lru_branch(p, conv_blk + 2, lru_pre, h0, lru_conv_w, lcb_r, wa_b, ba_r, wx_b, bx_r,
                                            lam_r, lru_g_r, layer, n_seq, t, t_valid, lru_chunk)
        return p, p[:, d_att:2 * d_att], p[:, 2 * d_att:3 * d_att], o_conv, o_lru, st_conv, st_lru, h_last

    def finish(x, layer, att, o_conv, o_lru):
        x = _mix_out(x, att, attn_g_r, o_conv, o_lru, w_out_b, layer, _col_tile(d_model, 1024))
        act = _norm_gateup(x, norm_ffn_r, w_gate_b, w_up_b, layer, _col_tile(w_gate.shape[-1], 1408))
        return _res_matmul(x, act, w_down_b, layer, _col_tile(d_model, 1024))

    prompt_states, sample_states = [], []
    for layer in range(depth):
        p, k, v, o_conv, o_lru, st_conv, st_lru, h_last = mixers(
            xp, layer, bp, t_p, t_p, zero_conv, zero_lru, zero_h, lru_chunk_p)
        att = _sb_prompt(p, sb_bias, suffix, layer, bp, t_p, n_meta, n_heads, head_dim)
        xp = finish(xp, layer, att, o_conv, o_lru)
        prompt_states.append((k.reshape(bp, t_p, n_heads, head_dim), v.reshape(bp, t_p, n_heads, head_dim),
                              st_conv, st_lru, h_last.reshape(bp, d_lru)))

        p, k, v, o_conv, o_lru, st_conv, st_lru, h_last = mixers(
            xs, layer, bs, t_s, dec_seq, state_conv[layer], state_lru_conv[layer],
            state_lru_h[layer].reshape(bs, 1, d_lru), t_s)
        q4 = p[:, :d_att].astype(BF16).reshape(bs, t_s, n_heads, head_dim).transpose(0, 2, 1, 3)
        pad_new = lambda a: jnp.pad(a.reshape(bs, t_s, d_att)[:, :dec_seq], ((0, 0), (0, KEY_TILE - dec_seq), (0, 0))
                                    ).reshape(bs, KEY_TILE * n_heads, head_dim)
        att = _sb_sample(q4, pad_new(k), pad_new(v), cache_k2, cache_v2, page_table, sb_bias, suffix, layer, head_dim)
        att = att.reshape(bs, n_heads, t_s, head_dim).transpose(0, 2, 1, 3).reshape(bs * t_s, d_att)
        xs = finish(xs, layer, att, o_conv, o_lru)
        new = lambda a: a.reshape(bs, t_s, n_heads, head_dim)[:, :dec_seq]
        sample_states.append((new(k), new(v), st_conv, st_lru, h_last.reshape(bs, d_lru)))

    stack = lambda states: [jnp.stack([s[i] for s in states], axis=0) for i in range(5)]
    kp, vp, cp, lcp, hp = stack(prompt_states)
    ksm, vsm, csm, lcsm, hsm = stack(sample_states)
    fin = final_norm.reshape(1, d_model)
    y_prompt = _final_norm(xp.reshape(bp, t_p, d_model)[:, n_meta:].reshape(bp * seq, d_model), fin)
    y_sample = _final_norm(xs, fin).reshape(bs, t_s, d_model)[:, :dec_seq]
    return (y_prompt.reshape(bp, seq, d_model), y_sample, kp, vp, cp, lcp, hp, ksm, vsm, csm, lcsm, hsm)
```

```python
import functools

import jax
import jax.numpy as jnp
from jax import lax
from jax.experimental import pallas as pl
from jax.experimental.pallas import tpu as pltpu

F32 = jnp.float32
BF16 = jnp.bfloat16

EPS = 1e-6
LRU_C = 8.0
LANES = 128
KEY_TILE = 128
SAMPLE_ROWS = 16
CONV_PAD = 32
LRU_PAD = 8
VMEM_LIMIT_BYTES = 56 * 1024 * 1024


def _params(*sem):
    return pltpu.CompilerParams(dimension_semantics=sem, vmem_limit_bytes=VMEM_LIMIT_BYTES)


def _row_tile(m, cap):
    if m <= cap:
        return m
    best = None
    for t in range(16, cap + 1, 16):
        if m % t == 0:
            best = t
    assert best is not None, m
    return best


def _col_tile(n, cap):
    best = max(t for t in range(LANES, min(n, cap) + 1, LANES) if n % t == 0)
    return best


def _rms(x, g):
    return x * lax.rsqrt(jnp.mean(x * x, axis=-1, keepdims=True) + EPS) * g


def _softplus(x):
    return jnp.maximum(x, 0.0) + jnp.log1p(jnp.exp(-jnp.abs(x)))


def _in_proj_body(x_ref, g_ref, w_ref, o_ref, k_ref, v_ref, u_ref, *, copies):
    j = pl.program_id(1)

    @pl.when(j == 0)
    def _():
        u_ref[...] = _rms(x_ref[...], g_ref[...]).astype(BF16)

    res = jnp.dot(u_ref[...], w_ref[...], preferred_element_type=F32)
    o_ref[...] = res
    for ref, (step, off) in zip((k_ref, v_ref), copies):
        @pl.when(j == step)
        def _(ref=ref, off=off):
            ref[...] = res[:, off:off + ref.shape[1]]


def _in_proj(x, g, w, layer, tn_cap, d_att):
    m, k = x.shape
    n = w.shape[-1]
    tm = _row_tile(m, 688)
    tn = max(t for t in range(LANES, min(n, tn_cap) + 1, LANES)
             if n % t == 0 and all(c0 % t + d_att <= t for c0 in (d_att, 2 * d_att)))
    copies = [divmod(c0, tn) for c0 in (d_att, 2 * d_att)]
    own = pl.BlockSpec((tm, d_att), lambda i, j: (i, 0))
    return pl.pallas_call(
        functools.partial(_in_proj_body, copies=copies),
        grid=(m // tm, n // tn),
        in_specs=[
            pl.BlockSpec((tm, k), lambda i, j: (i, 0)),
            pl.BlockSpec((None, 1, k), lambda i, j: (layer, 0, 0)),
            pl.BlockSpec((None, k, tn), lambda i, j: (layer, 0, j)),
        ],
        out_specs=[pl.BlockSpec((tm, tn), lambda i, j: (i, j)), own, own],
        out_shape=[jax.ShapeDtypeStruct((m, n), F32)] + [jax.ShapeDtypeStruct((m, d_att), F32)] * 2,
        scratch_shapes=[pltpu.VMEM((tm, k), BF16)],
        compiler_params=_params("parallel", "arbitrary"),
        name="in_proj",
    )(x, g, w)


def _norm_gateup_body(x_ref, g_ref, wg_ref, wu_ref, o_ref, u_ref):
    @pl.when(pl.program_id(1) == 0)
    def _():
        u_ref[...] = _rms(x_ref[...], g_ref[...]).astype(BF16)

    u = u_ref[...]
    gate = jnp.dot(u, wg_ref[...], preferred_element_type=F32)
    up = jnp.dot(u, wu_ref[...], preferred_element_type=F32)
    o_ref[...] = (gate * jax.nn.sigmoid(gate) * up).astype(o_ref.dtype)


def _norm_gateup(x, g, wg, wu, layer, tn):
    m, k = x.shape
    n = wg.shape[-1]
    tm = _row_tile(m, 688)
    return pl.pallas_call(
        _norm_gateup_body,
        grid=(m // tm, n // tn),
        in_specs=[
            pl.BlockSpec((tm, k), lambda i, j: (i, 0)),
            pl.BlockSpec((None, 1, k), lambda i, j: (layer, 0, 0)),
            pl.BlockSpec((None, k, tn), lambda i, j: (layer, 0, j)),
            pl.BlockSpec((None, k, tn), lambda i, j: (layer, 0, j)),
        ],
        out_specs=pl.BlockSpec((tm, tn), lambda i, j: (i, j)),
        out_shape=jax.ShapeDtypeStruct((m, n), BF16),
        scratch_shapes=[pltpu.VMEM((tm, k), BF16)],
        compiler_params=_params("parallel", "arbitrary"),
        name="ffn_gate_up",
    )(x, g, wg, wu)


def _res_mm_body(r_ref, a_ref, w_ref, o_ref):
    o_ref[...] = r_ref[...] + jnp.dot(a_ref[...], w_ref[...], preferred_element_type=F32)


def _res_matmul(res, a, w, layer, tn):
    m, k = a.shape
    n = w.shape[-1]
    tm = _row_tile(m, 688)
    return pl.pallas_call(
        _res_mm_body,
        grid=(m // tm, n // tn),
        in_specs=[
            pl.BlockSpec((tm, tn), lambda i, j: (i, j)),
            pl.BlockSpec((tm, k), lambda i, j: (i, 0)),
            pl.BlockSpec((None, k, tn), lambda i, j: (layer, 0, j)),
        ],
        out_specs=pl.BlockSpec((tm, tn), lambda i, j: (i, j)),
        out_shape=jax.ShapeDtypeStruct((m, n), F32),
        compiler_params=_params("parallel", "arbitrary"),
        name="ffn_down",
    )(res, a, w)


def _mix_out_body(r_ref, att_ref, ag_ref, conv_ref, lru_ref, w_ref, o_ref, mixed_ref):
    @pl.when(pl.program_id(1) == 0)
    def _():
        d_att, d_conv = att_ref.shape[1], conv_ref.shape[1]
        mixed_ref[:, :d_att] = _rms(att_ref[...], ag_ref[...]).astype(BF16)
        mixed_ref[:, d_att:d_att + d_conv] = conv_ref[...]
        mixed_ref[:, d_att + d_conv:] = lru_ref[...]

    o_ref[...] = r_ref[...] + jnp.dot(mixed_ref[...], w_ref[...], preferred_element_type=F32)


def _mix_out(res, att, att_g, conv, lru, w, layer, tn):
    m, n = res.shape
    d_att, d_conv, d_lru = att.shape[1], conv.shape[1], lru.shape[1]
    tm = _row_tile(m, 688)
    d_mix = d_att + d_conv + d_lru
    assert w.shape[1] == d_mix and d_att % LANES == 0 and d_conv % LANES == 0
    return pl.pallas_call(
        _mix_out_body,
        grid=(m // tm, n // tn),
        in_specs=[
            pl.BlockSpec((tm, tn), lambda i, j: (i, j)),
            pl.BlockSpec((tm, d_att), lambda i, j: (i, 0)),
            pl.BlockSpec((None, 1, d_att), lambda i, j: (layer, 0, 0)),
            pl.BlockSpec((tm, d_conv), lambda i, j: (i, 0)),
            pl.BlockSpec((tm, d_lru), lambda i, j: (i, 0)),
            pl.BlockSpec((None, d_mix, tn), lambda i, j: (layer, 0, j)),
        ],
        out_specs=pl.BlockSpec((tm, tn), lambda i, j: (i, j)),
        out_shape=jax.ShapeDtypeStruct((m, n), F32),
        scratch_shapes=[pltpu.VMEM((tm, d_mix), BF16)],
        compiler_params=_params("parallel", "arbitrary"),
        name="mix_out",
    )(res, att, att_g, conv, lru, w)


def _prepend_meta_body(meta_ref, x_ref, o_ref):
    n_meta = meta_ref.shape[0]
    o_ref[:n_meta, :] = meta_ref[...]
    o_ref[n_meta:, :] = x_ref[...]


def _prepend_meta(meta, x):
    bsz, seq, d = x.shape
    n_meta = meta.shape[0]
    tn = _col_tile(d, 512)
    assert n_meta % 8 == 0
    return pl.pallas_call(
        _prepend_meta_body,
        grid=(bsz, d // tn),
        in_specs=[pl.BlockSpec((n_meta, tn), lambda b, j: (0, j)), pl.BlockSpec((None, seq, tn), lambda b, j: (b, 0, j))],
        out_specs=pl.BlockSpec((None, n_meta + seq, tn), lambda b, j: (b, 0, j)),
        out_shape=jax.ShapeDtypeStruct((bsz, n_meta + seq, d), x.dtype),
        compiler_params=_params("parallel", "parallel"),
        name="prepend_meta",
    )(meta, x)


def _final_norm_body(x_ref, g_ref, o_ref):
    o_ref[...] = _rms(x_ref[...], g_ref[...])


def _final_norm(x, g):
    m, d = x.shape
    tm = _row_tile(m, 512)
    return pl.pallas_call(
        _final_norm_body,
        grid=(m // tm,),
        in_specs=[pl.BlockSpec((tm, d), lambda i: (i, 0)), pl.BlockSpec((1, d), lambda i: (0, 0))],
        out_specs=pl.BlockSpec((tm, d), lambda i: (i, 0)),
        out_shape=jax.ShapeDtypeStruct((m, d), F32),
        compiler_params=_params("parallel"),
        name="final_norm",
    )(x, g)


def _suffix_matrix():
    s = jnp.arange(KEY_TILE)[:, None]
    j = jnp.arange(KEY_TILE)[None, :]
    half = jnp.concatenate([(s > j).astype(BF16), jnp.ones((KEY_TILE, KEY_TILE), BF16)], axis=1)
    return jnp.concatenate([half, half], axis=0)


def _sb_group(z, mask, suffix, carry):
    n_tiles = z.shape[1] // KEY_TILE
    log_beta = jnp.minimum(z, 0.0) - jnp.log(1.0 + jnp.exp(-jnp.abs(z)))
    log_rest = log_beta - z
    if mask is not None:
        log_rest = jnp.where(mask, log_rest, 0.0)
    hi = log_rest.astype(BF16)
    lo = (log_rest - hi.astype(F32)).astype(BF16)
    sums = []
    for t in range(n_tiles):
        cols = slice(t * KEY_TILE, (t + 1) * KEY_TILE)
        sums.append(jnp.dot(jnp.concatenate([hi[:, cols], lo[:, cols]], axis=1), suffix,
                            preferred_element_type=F32))
    later = [None] * n_tiles
    for t in reversed(range(n_tiles)):
        later[t] = carry + sums[t][:, :KEY_TILE]
        carry = carry + sums[t][:, KEY_TILE:]
    a = jnp.exp(log_beta + jnp.concatenate(later, axis=1))
    if mask is not None:
        a = jnp.where(mask, a, 0.0)
    return a, carry


def _sb_prompt_body(bias_ref, q_ref, k_ref, v_ref, suf_ref, o_ref, qb_ref, kb_ref, vb_ref, carry_ref,
                    *, n_meta, q_rows, n_chunks, scale):
    bias = bias_ref[pl.program_id(1)]
    qb_ref[...] = q_ref[...].astype(BF16)
    kb_ref[...] = k_ref[...].astype(BF16)
    vb_ref[...] = v_ref[...].astype(BF16)
    suffix = suf_ref[...]
    half = q_rows // 2

    def weights(qb, ks, n_keys, mask, carry):
        kt = kb_ref[pl.ds(ks, n_keys), :]
        s = lax.dot_general(qb, kt, (((1,), (1,)), ((), ())), preferred_element_type=F32)
        return _sb_group(s * scale + bias, mask, suffix, carry)

    def q_chunk(c, _):
        qs = pl.multiple_of(n_meta + c * q_rows, 16)

        def group(r0, rows, ks, n_keys, mask):
            a, carry = weights(qb_ref[pl.ds(qs + r0, rows), :], ks, n_keys, mask, carry_ref[pl.ds(r0, rows), :])
            carry_ref[pl.ds(r0, rows), :] = carry
            o_ref[pl.ds(qs + r0, rows), :] += jnp.dot(a.astype(BF16), vb_ref[pl.ds(ks, n_keys), :],
                                                      preferred_element_type=F32)

        carry_ref[...] = jnp.zeros_like(carry_ref)
        o_ref[pl.ds(qs, q_rows), :] = jnp.zeros((q_rows, KEY_TILE), F32)
        for r0, n_keys in ((half, q_rows), (0, half)):
            row = lax.broadcasted_iota(jnp.int32, (half, n_keys), 0)
            col = lax.broadcasted_iota(jnp.int32, (half, n_keys), 1)
            group(r0, half, qs, n_keys, col < row + r0)

        def k_step(i, _):
            group(0, q_rows, pl.multiple_of(n_meta + (c - 1 - i) * q_rows, 16), q_rows, None)
            return 0

        lax.fori_loop(0, c, k_step, 0)
        col = lax.broadcasted_iota(jnp.int32, (q_rows, KEY_TILE), 1)
        group(0, q_rows, 0, KEY_TILE, col < n_meta)
        return 0

    lax.fori_loop(0, n_chunks, q_chunk, 0)
    row = lax.broadcasted_iota(jnp.int32, (KEY_TILE, KEY_TILE), 0)
    col = lax.broadcasted_iota(jnp.int32, (KEY_TILE, KEY_TILE), 1)
    a, _ = weights(qb_ref[pl.ds(0, KEY_TILE), :], 0, KEY_TILE, (row > col) & (col < n_meta),
                   jnp.zeros((KEY_TILE, KEY_TILE), F32))
    acc = jnp.dot(a.astype(BF16), vb_ref[pl.ds(0, KEY_TILE), :], preferred_element_type=F32)
    o_ref[pl.ds(0, n_meta), :] = acc[:n_meta]


def _sb_prompt(p, bias, suffix, layer, bsz, t, n_meta, n_heads, head_dim):
    m, d_att = p.shape[0], n_heads * head_dim
    seq = t - n_meta
    assert head_dim == LANES and n_meta % 16 == 0 and seq % (2 * KEY_TILE) == 0
    q_rows = 4 * KEY_TILE if seq % (4 * KEY_TILE) == 0 else 2 * KEY_TILE
    body = functools.partial(_sb_prompt_body, n_meta=n_meta, q_rows=q_rows, n_chunks=seq // q_rows,
                             scale=head_dim ** -0.5)
    blk = lambda part: pl.BlockSpec((t, head_dim), lambda b, h: (b, part * n_heads + h))
    return pl.pallas_call(
        body,
        grid=(bsz, n_heads),
        in_specs=[
            pl.BlockSpec(memory_space=pltpu.SMEM),
            blk(0), blk(1), blk(2),
            pl.BlockSpec((2 * KEY_TILE, 2 * KEY_TILE), lambda b, h: (0, 0)),
        ],
        out_specs=blk(0),
        out_shape=jax.ShapeDtypeStruct((m, d_att), F32),
        scratch_shapes=[pltpu.VMEM((t, head_dim), BF16)] * 3 + [pltpu.VMEM((q_rows, KEY_TILE), F32)],
        compiler_params=_params("parallel", "parallel"),
        name="sb_prompt",
    )(bias[layer], p, p, p, suffix)


def _sb_sample_body(pt_ref, bias_ref, q_ref, kn_ref, vn_ref, *rest, n_heads, group, scale):
    page_refs, (suf_ref, o_ref, carry_ref) = rest[:2 * group], rest[2 * group:]
    j = pl.program_id(1)
    rows = n_heads * SAMPLE_ROWS
    suffix = suf_ref[...]

    def process(k_refs, v_refs, mask, carry):
        zs = []
        for k_ref in k_refs:
            zt = []
            for h in range(n_heads):
                kh = k_ref[pl.ds(h, KEY_TILE, stride=n_heads), :].astype(BF16)
                s = lax.dot_general(q_ref[h], kh, (((1,), (1,)), ((), ())), preferred_element_type=F32)
                zt.append(s * scale + bias_ref[h])
            zs.append(jnp.concatenate(zt, axis=0))
        a, carry = _sb_group(jnp.concatenate(zs, axis=1), mask, suffix, carry)
        a = a.astype(BF16)
        outs = []
        for h in range(n_heads):
            rows_h = slice(h * SAMPLE_ROWS, (h + 1) * SAMPLE_ROWS)
            acc = None
            for t, v_ref in enumerate(v_refs):
                vh = v_ref[pl.ds(h, KEY_TILE, stride=n_heads), :].astype(BF16)
                part = jnp.dot(a[rows_h, t * KEY_TILE:(t + 1) * KEY_TILE], vh, preferred_element_type=F32)
                acc = part if acc is None else acc + part
            outs.append(acc)
        return jnp.concatenate(outs, axis=0), carry

    @pl.when(j == 0)
    def _():
        row = lax.broadcasted_iota(jnp.int32, (rows, KEY_TILE), 0) % SAMPLE_ROWS
        col = lax.broadcasted_iota(jnp.int32, (rows, KEY_TILE), 1)
        out, carry = process([kn_ref], [vn_ref], col < row, jnp.zeros((rows, KEY_TILE), F32))
        o_ref[...] = out
        carry_ref[...] = carry

    @pl.when(j > 0)
    def _():
        out, carry = process(page_refs[:group][::-1], page_refs[group:][::-1], None, carry_ref[...])
        o_ref[...] += out
        carry_ref[...] = carry


def _sb_sample(q, k_new, v_new, cache_k, cache_v, page_table, bias, suffix, layer, head_dim):
    bd, n_heads = q.shape[:2]
    n_pages = page_table.shape[1]
    assert cache_k.shape[2] == KEY_TILE * n_heads and head_dim == LANES
    page_rows = KEY_TILE * n_heads
    group = max(g for g in (8, 4, 2, 1) if n_pages % g == 0)
    body = functools.partial(_sb_sample_body, n_heads=n_heads, group=group, scale=head_dim ** -0.5)

    def page_blk(g):
        def index(b, j, pt):
            return (layer, pt[b, n_pages - 1 - ((jnp.maximum(j, 1) - 1) * group + g)], 0, 0)
        return pl.BlockSpec((None, None, page_rows, head_dim), index)

    new_blk = pl.BlockSpec((None, page_rows, head_dim), lambda b, j, pt: (b, 0, 0))
    pages = [page_blk(g) for g in range(group)]
    return pl.pallas_call(
        body,
        grid_spec=pltpu.PrefetchScalarGridSpec(
            num_scalar_prefetch=1,
            grid=(bd, n_pages // group + 1),
            in_specs=[
                pl.BlockSpec(memory_space=pltpu.SMEM),
                pl.BlockSpec((None, n_heads, SAMPLE_ROWS, head_dim), lambda b, j, pt: (b, 0, 0, 0)),
                new_blk, new_blk, *pages, *pages,
                pl.BlockSpec((2 * KEY_TILE, 2 * KEY_TILE), lambda b, j, pt: (0, 0)),
            ],
            out_specs=pl.BlockSpec((None, n_heads * SAMPLE_ROWS, head_dim), lambda b, j, pt: (b, 0, 0)),
            scratch_shapes=[pltpu.VMEM((n_heads * SAMPLE_ROWS, KEY_TILE), F32)],
        ),
        out_shape=jax.ShapeDtypeStruct((bd, n_heads * SAMPLE_ROWS, head_dim), F32),
        compiler_params=_params("parallel", "arbitrary"),
        name="sb_sample",
    )(page_table, bias[layer], q, k_new, v_new, *([cache_k] * group), *([cache_v] * group), suffix)


def _conv_body(a_ref, g_ref, pre_ref, w_ref, b_ref, lg_ref, lb_ref, o_ref, st_ref, ext_ref, sh_ref,
               *, t, t_valid, width, chunk):
    base = CONV_PAD - (width - 1)
    ext_ref[pl.ds(base, width - 1), :] = pre_ref[...]
    n_chunks = t // chunk

    def glu(i, _):
        r = pl.ds(pl.multiple_of(i * chunk, 8), chunk)
        u = a_ref[r, :] * jax.nn.sigmoid(g_ref[r, :])
        ext_ref[pl.ds(pl.multiple_of(CONV_PAD + i * chunk, 8), chunk), :] = u
        return 0

    lax.fori_loop(0, n_chunks, glu, 0)
    st_ref[...] = ext_ref[pl.ds(CONV_PAD + t_valid - (width - 1), width - 1), :]

    def conv(i, _):
        t0 = pl.multiple_of(i * chunk, 8)
        win = ext_ref[pl.ds(t0, chunk + CONV_PAD), :]
        for r in range(1, 8):
            sh_ref[r] = win[r:r + chunk + CONV_PAD - 8]
        acc = jnp.zeros((chunk, a_ref.shape[-1]), F32) + b_ref[...]
        for w in range(width):
            q, r = divmod(base + w, 8)
            if r == 0:
                tap = ext_ref[pl.ds(pl.multiple_of(t0 + 8 * q, 8), chunk), :]
            else:
                tap = sh_ref[r, pl.ds(8 * q, chunk), :]
            acc = acc + tap * w_ref[pl.ds(w, 1), :]
        mean = jnp.mean(acc, axis=-1, keepdims=True)
        cen = acc - mean
        y = cen * lax.rsqrt(jnp.mean(cen * cen, axis=-1, keepdims=True) + EPS) * lg_ref[...] + lb_ref[...]
        o_ref[pl.ds(t0, chunk), :] = (y * jax.nn.sigmoid(y)).astype(o_ref.dtype)
        return 0

    lax.fori_loop(0, n_chunks, conv, 0)


def _conv_branch(p, col_blk, prefix, w, b, ln_g, ln_b, layer, n_seq, t, t_valid):
    width, c = w.shape[1:]
    chunk = _row_tile(t, 48)
    assert t % chunk == 0 and width - 1 <= CONV_PAD and 0 < t_valid <= t
    body = functools.partial(_conv_body, t=t, t_valid=t_valid, width=width, chunk=chunk)
    vec = pl.BlockSpec((None, 1, c), lambda s: (layer, 0, 0))
    return pl.pallas_call(
        body,
        grid=(n_seq,),
        in_specs=[
            pl.BlockSpec((t, c), lambda s: (s, col_blk)),
            pl.BlockSpec((t, c), lambda s: (s, col_blk + 1)),
            pl.BlockSpec((None, width - 1, c), lambda s: (s, 0, 0)),
            pl.BlockSpec((None, width, c), lambda s: (layer, 0, 0)),
            vec, vec, vec,
        ],
        out_specs=[
            pl.BlockSpec((t, c), lambda s: (s, 0)),
            pl.BlockSpec((None, width - 1, c), lambda s: (s, 0, 0)),
        ],
        out_shape=[
            jax.ShapeDtypeStruct((n_seq * t, c), BF16),
            jax.ShapeDtypeStruct((n_seq, width - 1, c), F32),
        ],
        scratch_shapes=[pltpu.VMEM((CONV_PAD + t, c), F32), pltpu.VMEM((8, chunk + CONV_PAD - 8, c), F32)],
        compiler_params=_params("parallel"),
        name="conv_branch",
    )(p, p, prefix, w, b, ln_g, ln_b)


def _lru_body(x_ref, gate_ref, pre_ref, h0_ref, cw_ref, cb_ref, wa_ref, ba_ref, wx_ref, bx_ref, lam_ref, og_ref,
              o_ref, st_ref, hl_ref, ext_ref, a_ref, u_ref, *, t, t_valid, width, chunk):
    base = LRU_PAD - (width - 1)
    ext_ref[pl.ds(base, width - 1), :] = pre_ref[...]
    n_chunks = t // chunk

    def fill(i, _):
        t0 = pl.multiple_of(i * chunk, 8)
        ext_ref[pl.ds(LRU_PAD + t0, chunk), :] = x_ref[pl.ds(t0, chunk), :]
        return 0

    lax.fori_loop(0, n_chunks, fill, 0)
    st_ref[...] = ext_ref[pl.ds(LRU_PAD + t_valid - (width - 1), width - 1), :]
    neg_c_sp = -LRU_C * _softplus(-lam_ref[...])

    def gates(i, _):
        t0 = pl.multiple_of(i * chunk, 8)
        win = ext_ref[pl.ds(t0, chunk + LRU_PAD), :]
        xc = jnp.zeros((chunk, x_ref.shape[-1]), F32) + cb_ref[...]
        for w in range(width):
            xc = xc + win[base + w:base + w + chunk] * cw_ref[pl.ds(w, 1), :]
        xb = xc.astype(BF16)
        r = jax.nn.sigmoid(jnp.dot(xb, wa_ref[...], preferred_element_type=F32) + ba_ref[...])
        ig = jax.nn.sigmoid(jnp.dot(xb, wx_ref[...], preferred_element_type=F32) + bx_ref[...])
        log_a = r * neg_c_sp
        a = jnp.exp(log_a)
        a_ref[pl.ds(t0, chunk), :] = a
        u_ref[pl.ds(t0, chunk), :] = jnp.sqrt(-jnp.tanh(log_a) * (1.0 + a * a)) * (ig * xc)
        return 0

    lax.fori_loop(0, n_chunks, gates, 0)

    def step(s, h):
        h = a_ref[pl.ds(s, 1), :] * h + u_ref[pl.ds(s, 1), :]
        u_ref[pl.ds(s, 1), :] = h
        return h

    hl_ref[...] = lax.fori_loop(0, t_valid, step, h0_ref[...], unroll=min(8, t_valid))

    def out(i, _):
        r = pl.ds(pl.multiple_of(i * chunk, 8), chunk)
        y = u_ref[r, :] * jax.nn.gelu(gate_ref[r, :])
        o_ref[r, :] = _rms(y, og_ref[...]).astype(o_ref.dtype)
        return 0

    lax.fori_loop(0, n_chunks, out, 0)


def _lru_branch(p, col_blk, prefix, h0, cw, cb, wa, ba, wx, bx, lam, out_g, layer, n_seq, t, t_valid, chunk):
    width, c = cw.shape[1:]
    assert t % chunk == 0 and chunk % 8 == 0 and width - 1 <= LRU_PAD
    body = functools.partial(_lru_body, t=t, t_valid=t_valid, width=width, chunk=chunk)
    vec = pl.BlockSpec((None, 1, c), lambda s: (layer, 0, 0))
    mat = pl.BlockSpec((None, c, c), lambda s: (layer, 0, 0))
    return pl.pallas_call(
        body,
        grid=(n_seq,),
        in_specs=[
            pl.BlockSpec((t, c), lambda s: (s, col_blk)),
            pl.BlockSpec((t, c), lambda s: (s, col_blk + 1)),
            pl.BlockSpec((None, width - 1, c), lambda s: (s, 0, 0)),
            pl.BlockSpec((None, 1, c), lambda s: (s, 0, 0)),
            pl.BlockSpec((None, width, c), lambda s: (layer, 0, 0)),
            vec, mat, vec, mat, vec, vec, vec,
        ],
        out_specs=[
            pl.BlockSpec((t, c), lambda s: (s, 0)),
            pl.BlockSpec((None, width - 1, c), lambda s: (s, 0, 0)),
            pl.BlockSpec((None, 1, c), lambda s: (s, 0, 0)),
        ],
        out_shape=[
            jax.ShapeDtypeStruct((n_seq * t, c), BF16),
            jax.ShapeDtypeStruct((n_seq, width - 1, c), F32),
            jax.ShapeDtypeStruct((n_seq, 1, c), F32),
        ],
        scratch_shapes=[pltpu.VMEM((LRU_PAD + t, c), F32), pltpu.VMEM((t, c), F32), pltpu.VMEM((t, c), F32)],
        compiler_params=_params("parallel"),
        name="lru_branch",
    )(p, p, prefix, h0, cw, cb, wa, ba, wx, bx, lam, out_g)


def _block_diag(w):
    depth, nb, bi, bj = w.shape
    eye = jnp.eye(nb, dtype=w.dtype)
    return jnp.einsum("lnij,nm->lnimj", w, eye).reshape(depth, nb * bi, nb * bj)


def kernel(x_prompt, x_sample, cache_k, cache_v, state_conv, state_lru_conv, state_lru_h, page_table, meta_tokens, norm_mix, w_in, sb_bias, conv_w, conv_b, conv_ln_g, conv_ln_b, lru_conv_w, lru_conv_b, lru_wa, lru_ba, lru_wx, lru_bx, lru_lambda, attn_out_g, lru_out_g, w_out, norm_ffn, w_gate, w_up, w_down, final_norm):
    bp, seq, d_model = x_prompt.shape
    bs, dec_seq, _ = x_sample.shape
    depth, n_pool, page, n_heads, head_dim = cache_k.shape
    n_meta = meta_tokens.shape[0]
    d_att = n_heads * head_dim
    d_conv = conv_w.shape[-1]
    d_lru = lru_conv_w.shape[-1]
    conv_width = conv_w.shape[1]
    lru_width = lru_conv_w.shape[1]
    t_p = n_meta + seq
    t_s = SAMPLE_ROWS
    assert dec_seq <= SAMPLE_ROWS and d_conv == d_lru and d_att % d_conv == 0
    assert w_in.shape[-1] == 3 * d_att + 2 * d_conv + 2 * d_lru

    w_in_b, w_out_b = w_in.astype(BF16), w_out.astype(BF16)
    w_gate_b, w_up_b, w_down_b = w_gate.astype(BF16), w_up.astype(BF16), w_down.astype(BF16)
    wa_b, wx_b = _block_diag(lru_wa).astype(BF16), _block_diag(lru_wx).astype(BF16)
    row = lambda v: v.reshape(depth, 1, -1)
    norm_mix_r, norm_ffn_r, attn_g_r, lru_g_r = row(norm_mix), row(norm_ffn), row(attn_out_g), row(lru_out_g)
    conv_b_r, ln_g_r, ln_b_r = row(conv_b), row(conv_ln_g), row(conv_ln_b)
    lcb_r, ba_r, bx_r, lam_r = row(lru_conv_b), row(lru_ba), row(lru_bx), row(lru_lambda)
    suffix = _suffix_matrix()
    cache_k2 = cache_k.reshape(depth, n_pool, page * n_heads, head_dim)
    cache_v2 = cache_v.reshape(depth, n_pool, page * n_heads, head_dim)

    xp = _prepend_meta(meta_tokens.astype(x_prompt.dtype), x_prompt).reshape(bp * t_p, d_model)
    xs = jnp.pad(x_sample, ((0, 0), (0, t_s - dec_seq), (0, 0))).reshape(bs * t_s, d_model)

    zero_conv = jnp.zeros((bp, conv_width - 1, d_conv), F32)
    zero_lru = jnp.zeros((bp, lru_width - 1, d_lru), F32)
    zero_h = jnp.zeros((bp, 1, d_lru), F32)
    conv_blk = 3 * d_att // d_conv
    lru_chunk_p = _row_tile(t_p, 688)

    d_in = w_in.shape[-1]

    def mixers(x, layer, n_seq, t, t_valid, conv_pre, lru_pre, h0, lru_chunk):
        p, k, v = _in_proj(x, norm_mix_r, w_in_b, layer, 2048, d_att)
        o_conv, st_conv = _conv_branch(p, conv_blk, conv_pre, conv_w, conv_b_r, ln_g_r, ln_b_r, layer, n_seq, t, t_valid)
        o_lru, st_lru, h_last = _lru_branch(p, conv_blk + 2, lru_pre, h0, lru_conv_w, lcb_r, wa_b, ba_r, wx_b, bx_r,
                                            lam_r, lru_g_r, layer, n_seq, t, t_valid, lru_chunk)
        return p, k, v, o_conv, o_lru, st_conv, st_lru, h_last

    def finish(x, layer, att, o_conv, o_lru):
        x = _mix_out(x, att, attn_g_r, o_conv, o_lru, w_out_b, layer, _col_tile(d_model, 1024))
        act = _norm_gateup(x, norm_ffn_r, w_gate_b, w_up_b, layer, _col_tile(w_gate.shape[-1], 1408))
        return _res_matmul(x, act, w_down_b, layer, _col_tile(d_model, 1024))

    prompt_states, sample_states = [], []
    for layer in range(depth):
        p, k, v, o_conv, o_lru, st_conv, st_lru, h_last = mixers(
            xp, layer, bp, t_p, t_p, zero_conv, zero_lru, zero_h, lru_chunk_p)
        att = _sb_prompt(p, sb_bias, suffix, layer, bp, t_p, n_meta, n_heads, head_dim)
        xp = finish(xp, layer, att, o_conv, o_lru)
        prompt_states.append((k.reshape(bp, t_p, n_heads, head_dim), v.reshape(bp, t_p, n_heads, head_dim),
                              st_conv, st_lru, h_last.reshape(bp, d_lru)))

        p, k, v, o_conv, o_lru, st_conv, st_lru, h_last = mixers(
            xs, layer, bs, t_s, dec_seq, state_conv[layer], state_lru_conv[layer],
            state_lru_h[layer].reshape(bs, 1, d_lru), t_s)
        q4 = p[:, :d_att].astype(BF16).reshape(bs, t_s, n_heads, head_dim).transpose(0, 2, 1, 3)
        pad_new = lambda a: jnp.pad(a.reshape(bs, t_s, d_att)[:, :dec_seq], ((0, 0), (0, KEY_TILE - dec_seq), (0, 0))
                                    ).reshape(bs, KEY_TILE * n_heads, head_dim)
        att = _sb_sample(q4, pad_new(k), pad_new(v), cache_k2, cache_v2, page_table, sb_bias, suffix, layer, head_dim)
        att = att.reshape(bs, n_heads, t_s, head_dim).transpose(0, 2, 1, 3).reshape(bs * t_s, d_att)
        xs = finish(xs, layer, att, o_conv, o_lru)
        new = lambda a: a.reshape(bs, t_s, n_heads, head_dim)[:, :dec_seq]
        sample_states.append((new(k), new(v), st_conv, st_lru, h_last.reshape(bs, d_lru)))

    stack = lambda states: [jnp.stack([s[i] for s in states], axis=0) for i in range(5)]
    kp, vp, cp, lcp, hp = stack(prompt_states)
    ksm, vsm, csm, lcsm, hsm = stack(sample_states)
    fin = final_norm.reshape(1, d_model)
    y_prompt = _final_norm(xp.reshape(bp, t_p, d_model)[:, n_meta:].reshape(bp * seq, d_model), fin)
    y_sample = _final_norm(xs, fin).reshape(bs, t_s, d_model)[:, :dec_seq]
    return (y_prompt.reshape(bp, seq, d_model), y_sample, kp, vp, cp, lcp, hp, ksm, vsm, csm, lcsm, hsm)
```

```python
import functools

import jax
import jax.numpy as jnp
from jax import lax
from jax.experimental import pallas as pl
from jax.experimental.pallas import tpu as pltpu

F32 = jnp.float32
BF16 = jnp.bfloat16

EPS = 1e-6
LRU_C = 8.0
LANES = 128
KEY_TILE = 128
SAMPLE_ROWS = 16
CONV_PAD = 32
LRU_PAD = 8
VMEM_LIMIT_BYTES = 56 * 1024 * 1024


def _params(*sem):
    return pltpu.CompilerParams(dimension_semantics=sem, vmem_limit_bytes=VMEM_LIMIT_BYTES)


def _row_tile(m, cap):
    if m <= cap:
        return m
    best = None
    for t in range(16, cap + 1, 16):
        if m % t == 0:
            best = t
    assert best is not None, m
    return best


def _col_tile(n, cap):
    best = max(t for t in range(LANES, min(n, cap) + 1, LANES) if n % t == 0)
    return best


def _rms(x, g):
    return x * lax.rsqrt(jnp.mean(x * x, axis=-1, keepdims=True) + EPS) * g


def _softplus(x):
    return jnp.maximum(x, 0.0) + jnp.log1p(jnp.exp(-jnp.abs(x)))


def _in_proj_body(x_ref, g_ref, w_ref, o_ref, k_ref, v_ref, u_ref, *, copies):
    j = pl.program_id(1)

    @pl.when(j == 0)
    def _():
        u_ref[...] = _rms(x_ref[...], g_ref[...]).astype(BF16)

    res = jnp.dot(u_ref[...], w_ref[...], preferred_element_type=F32)
    o_ref[...] = res
    for ref, (step, off) in zip((k_ref, v_ref), copies):
        @pl.when(j == step)
        def _(ref=ref, off=off):
            ref[...] = res[:, off:off + ref.shape[1]]


def _in_proj(x, g, w, layer, tn_cap, d_att):
    m, k = x.shape
    n = w.shape[-1]
    tm = _row_tile(m, 688)
    tn = max(t for t in range(LANES, min(n, tn_cap) + 1, LANES)
             if n % t == 0 and all(c0 % t + d_att <= t for c0 in (d_att, 2 * d_att)))
    copies = [divmod(c0, tn) for c0 in (d_att, 2 * d_att)]
    own = pl.BlockSpec((tm, d_att), lambda i, j: (i, 0))
    return pl.pallas_call(
        functools.partial(_in_proj_body, copies=copies),
        grid=(m // tm, n // tn),
        in_specs=[
            pl.BlockSpec((tm, k), lambda i, j: (i, 0)),
            pl.BlockSpec((None, 1, k), lambda i, j: (layer, 0, 0)),
            pl.BlockSpec((None, k, tn), lambda i, j: (layer, 0, j)),
        ],
        out_specs=[pl.BlockSpec((tm, tn), lambda i, j: (i, j)), own, own],
        out_shape=[jax.ShapeDtypeStruct((m, n), F32)] + [jax.ShapeDtypeStruct((m, d_att), F32)] * 2,
        scratch_shapes=[pltpu.VMEM((tm, k), BF16)],
        compiler_params=_params("parallel", "arbitrary"),
        name="in_proj",
    )(x, g, w)


def _norm_gateup_body(x_ref, g_ref, wg_ref, wu_ref, o_ref, u_ref):
    @pl.when(pl.program_id(1) == 0)
    def _():
        u_ref[...] = _rms(x_ref[...], g_ref[...]).astype(BF16)

    u = u_ref[...]
    gate = jnp.dot(u, wg_ref[...], preferred_element_type=F32)
    up = jnp.dot(u, wu_ref[...], preferred_element_type=F32)
    o_ref[...] = (gate * jax.nn.sigmoid(gate) * up).astype(o_ref.dtype)


def _norm_gateup(x, g, wg, wu, layer, tn):
    m, k = x.shape
    n = wg.shape[-1]
    tm = _row_tile(m, 688)
    return pl.pallas_call(
        _norm_gateup_body,
        grid=(m // tm, n // tn),
        in_specs=[
            pl.BlockSpec((tm, k), lambda i, j: (i, 0)),
            pl.BlockSpec((None, 1, k), lambda i, j: (layer, 0, 0)),
            pl.BlockSpec((None, k, tn), lambda i, j: (0, 0, j)),
            pl.BlockSpec((None, k, tn), lambda i, j: (0, 0, j)),
        ],
        out_specs=pl.BlockSpec((tm, tn), lambda i, j: (i, j)),
        out_shape=jax.ShapeDtypeStruct((m, n), BF16),
        scratch_shapes=[pltpu.VMEM((tm, k), BF16)],
        compiler_params=_params("parallel", "arbitrary"),
        name="ffn_gate_up",
    )(x, g, wg, wu)


def _res_mm_body(r_ref, a_ref, w_ref, o_ref):
    o_ref[...] = r_ref[...] + jnp.dot(a_ref[...], w_ref[...], preferred_element_type=F32)


def _res_matmul(res, a, w, layer, tn):
    m, k = a.shape
    n = w.shape[-1]
    tm = _row_tile(m, 688)
    return pl.pallas_call(
        _res_mm_body,
        grid=(m // tm, n // tn),
        in_specs=[
            pl.BlockSpec((tm, tn), lambda i, j: (i, j)),
            pl.BlockSpec((tm, k), lambda i, j: (i, 0)),
            pl.BlockSpec((None, k, tn), lambda i, j: (layer, 0, j)),
        ],
        out_specs=pl.BlockSpec((tm, tn), lambda i, j: (i, j)),
        out_shape=jax.ShapeDtypeStruct((m, n), F32),
        compiler_params=_params("parallel", "arbitrary"),
        name="ffn_down",
    )(res, a, w)


def _mix_out_body(r_ref, att_ref, ag_ref, conv_ref, lru_ref, w_ref, o_ref, mixed_ref):
    @pl.when(pl.program_id(1) == 0)
    def _():
        d_att, d_conv = att_ref.shape[1], conv_ref.shape[1]
        mixed_ref[:, :d_att] = _rms(att_ref[...], ag_ref[...]).astype(BF16)
        mixed_ref[:, d_att:d_att + d_conv] = conv_ref[...]
        mixed_ref[:, d_att + d_conv:] = lru_ref[...]

    o_ref[...] = r_ref[...] + jnp.dot(mixed_ref[...], w_ref[...], preferred_element_type=F32)


def _mix_out(res, att, att_g, conv, lru, w, layer, tn):
    m, n = res.shape
    d_att, d_conv, d_lru = att.shape[1], conv.shape[1], lru.shape[1]
    tm = _row_tile(m, 688)
    d_mix = d_att + d_conv + d_lru
    assert w.shape[1] == d_mix and d_att % LANES == 0 and d_conv % LANES == 0
    return pl.pallas_call(
        _mix_out_body,
        grid=(m // tm, n // tn),
        in_specs=[
            pl.BlockSpec((tm, tn), lambda i, j: (i, j)),
            pl.BlockSpec((tm, d_att), lambda i, j: (i, 0)),
            pl.BlockSpec((None, 1, d_att), lambda i, j: (layer, 0, 0)),
            pl.BlockSpec((tm, d_conv), lambda i, j: (i, 0)),
            pl.BlockSpec((tm, d_lru), lambda i, j: (i, 0)),
            pl.BlockSpec((None, d_mix, tn), lambda i, j: (0, 0, j)),
        ],
        out_specs=pl.BlockSpec((tm, tn), lambda i, j: (i, j)),
        out_shape=jax.ShapeDtypeStruct((m, n), F32),
        scratch_shapes=[pltpu.VMEM((tm, d_mix), BF16)],
        compiler_params=_params("parallel", "arbitrary"),
        name="mix_out",
    )(res, att, att_g, conv, lru, w)


def _prepend_meta_body(meta_ref, x_ref, o_ref):
    n_meta = meta_ref.shape[0]
    o_ref[:n_meta, :] = meta_ref[...]
    o_ref[n_meta:, :] = x_ref[...]


def _prepend_meta(meta, x):
    bsz, seq, d = x.shape
    n_meta = meta.shape[0]
    tn = _col_tile(d, 512)
    assert n_meta % 8 == 0
    return pl.pallas_call(
        _prepend_meta_body,
        grid=(bsz, d // tn),
        in_specs=[pl.BlockSpec((n_meta, tn), lambda b, j: (0, j)), pl.BlockSpec((None, seq, tn), lambda b, j: (b, 0, j))],
        out_specs=pl.BlockSpec((None, n_meta + seq, tn), lambda b, j: (b, 0, j)),
        out_shape=jax.ShapeDtypeStruct((bsz, n_meta + seq, d), x.dtype),
        compiler_params=_params("parallel", "parallel"),
        name="prepend_meta",
    )(meta, x)


def _final_norm_body(x_ref, g_ref, o_ref):
    o_ref[...] = _rms(x_ref[...], g_ref[...])


def _final_norm(x, g):
    m, d = x.shape
    tm = _row_tile(m, 512)
    return pl.pallas_call(
        _final_norm_body,
        grid=(m // tm,),
        in_specs=[pl.BlockSpec((tm, d), lambda i: (i, 0)), pl.BlockSpec((1, d), lambda i: (0, 0))],
        out_specs=pl.BlockSpec((tm, d), lambda i: (i, 0)),
        out_shape=jax.ShapeDtypeStruct((m, d), F32),
        compiler_params=_params("parallel"),
        name="final_norm",
    )(x, g)


def _suffix_matrix():
    s = jnp.arange(KEY_TILE)[:, None]
    j = jnp.arange(KEY_TILE)[None, :]
    half = jnp.concatenate([(s > j).astype(BF16), jnp.ones((KEY_TILE, KEY_TILE), BF16)], axis=1)
    return jnp.concatenate([half, half], axis=0)


def _sb_group(z, mask, suffix, carry):
    n_tiles = z.shape[1] // KEY_TILE
    log_beta = jnp.minimum(z, 0.0) - jnp.log(1.0 + jnp.exp(-jnp.abs(z)))
    log_rest = log_beta - z
    if mask is not None:
        log_rest = jnp.where(mask, log_rest, 0.0)
    hi = log_rest.astype(BF16)
    lo = (log_rest - hi.astype(F32)).astype(BF16)
    sums = []
    for t in range(n_tiles):
        cols = slice(t * KEY_TILE, (t + 1) * KEY_TILE)
        sums.append(jnp.dot(jnp.concatenate([hi[:, cols], lo[:, cols]], axis=1), suffix,
                            preferred_element_type=F32))
    later = [None] * n_tiles
    for t in reversed(range(n_tiles)):
        later[t] = carry + sums[t][:, :KEY_TILE]
        carry = carry + sums[t][:, KEY_TILE:]
    a = jnp.exp(log_beta + jnp.concatenate(later, axis=1))
    if mask is not None:
        a = jnp.where(mask, a, 0.0)
    return a, carry


def _sb_prompt_body(bias_ref, q_ref, k_ref, v_ref, suf_ref, *rest, n_cast, n_meta, q_rows, n_chunks, scale):
    cast_in, o_ref, cast_out = rest[:n_cast], rest[n_cast], rest[n_cast + 1:2 * n_cast + 1]
    qb_ref, kb_ref, vb_ref, carry_ref = rest[2 * n_cast + 1:]
    for w_ref, wb_ref in zip(cast_in, cast_out):
        wb_ref[...] = w_ref[...].astype(BF16)
    bias = bias_ref[pl.program_id(1)]
    qb_ref[...] = q_ref[...].astype(BF16)
    kb_ref[...] = k_ref[...].astype(BF16)
    vb_ref[...] = v_ref[...].astype(BF16)
    suffix = suf_ref[...]
    half = q_rows // 2

    def weights(qb, ks, n_keys, mask, carry):
        kt = kb_ref[pl.ds(ks, n_keys), :]
        s = lax.dot_general(qb, kt, (((1,), (1,)), ((), ())), preferred_element_type=F32)
        return _sb_group(s * scale + bias, mask, suffix, carry)

    def q_chunk(c, _):
        qs = pl.multiple_of(n_meta + c * q_rows, 16)

        def group(r0, rows, ks, n_keys, mask):
            a, carry = weights(qb_ref[pl.ds(qs + r0, rows), :], ks, n_keys, mask, carry_ref[pl.ds(r0, rows), :])
            carry_ref[pl.ds(r0, rows), :] = carry
            o_ref[pl.ds(qs + r0, rows), :] += jnp.dot(a.astype(BF16), vb_ref[pl.ds(ks, n_keys), :],
                                                      preferred_element_type=F32)

        carry_ref[...] = jnp.zeros_like(carry_ref)
        o_ref[pl.ds(qs, q_rows), :] = jnp.zeros((q_rows, KEY_TILE), F32)
        for r0, n_keys in ((half, q_rows), (0, half)):
            row = lax.broadcasted_iota(jnp.int32, (half, n_keys), 0)
            col = lax.broadcasted_iota(jnp.int32, (half, n_keys), 1)
            group(r0, half, qs, n_keys, col < row + r0)

        def k_step(i, _):
            group(0, q_rows, pl.multiple_of(n_meta + (c - 1 - i) * q_rows, 16), q_rows, None)
            return 0

        lax.fori_loop(0, c, k_step, 0)
        col = lax.broadcasted_iota(jnp.int32, (q_rows, KEY_TILE), 1)
        group(0, q_rows, 0, KEY_TILE, col < n_meta)
        return 0

    lax.fori_loop(0, n_chunks, q_chunk, 0)
    row = lax.broadcasted_iota(jnp.int32, (KEY_TILE, KEY_TILE), 0)
    col = lax.broadcasted_iota(jnp.int32, (KEY_TILE, KEY_TILE), 1)
    a, _ = weights(qb_ref[pl.ds(0, KEY_TILE), :], 0, KEY_TILE, (row > col) & (col < n_meta),
                   jnp.zeros((KEY_TILE, KEY_TILE), F32))
    acc = jnp.dot(a.astype(BF16), vb_ref[pl.ds(0, KEY_TILE), :], preferred_element_type=F32)
    o_ref[pl.ds(0, n_meta), :] = acc[:n_meta]


def _sb_prompt(p, bias, suffix, layer, bsz, t, n_meta, n_heads, head_dim, to_cast):
    m, d_att = p.shape[0], n_heads * head_dim
    seq = t - n_meta
    assert head_dim == LANES and n_meta % 16 == 0 and seq % (2 * KEY_TILE) == 0
    q_rows = 4 * KEY_TILE if seq % (4 * KEY_TILE) == 0 else 2 * KEY_TILE
    body = functools.partial(_sb_prompt_body, n_cast=len(to_cast), n_meta=n_meta, q_rows=q_rows,
                             n_chunks=seq // q_rows, scale=head_dim ** -0.5)
    blk = lambda part: pl.BlockSpec((t, head_dim), lambda b, h: (b, part * n_heads + h))
    steps = bsz * n_heads
    slabs = [w.shape[1] // steps for w in to_cast]
    assert all(w.shape[1] % steps == 0 and s % 16 == 0 for w, s in zip(to_cast, slabs))
    outs = pl.pallas_call(
        body,
        grid=(bsz, n_heads),
        in_specs=[
            pl.BlockSpec(memory_space=pltpu.SMEM),
            blk(0), blk(1), blk(2),
            pl.BlockSpec((2 * KEY_TILE, 2 * KEY_TILE), lambda b, h: (0, 0)),
            *[pl.BlockSpec((None, s, w.shape[2]), lambda b, h: (layer, b * n_heads + h, 0))
              for w, s in zip(to_cast, slabs)],
        ],
        out_specs=[blk(0)] + [pl.BlockSpec((None, s, w.shape[2]), lambda b, h: (0, b * n_heads + h, 0))
                              for w, s in zip(to_cast, slabs)],
        out_shape=[jax.ShapeDtypeStruct((m, d_att), F32)]
        + [jax.ShapeDtypeStruct((1,) + w.shape[1:], BF16) for w in to_cast],
        scratch_shapes=[pltpu.VMEM((t, head_dim), BF16)] * 3 + [pltpu.VMEM((q_rows, KEY_TILE), F32)],
        compiler_params=_params("parallel", "parallel"),
        name="sb_prompt",
    )(bias[layer], p, p, p, suffix, *to_cast)
    return outs[0], outs[1:]


def _sb_sample_body(pt_ref, bias_ref, q_ref, kn_ref, vn_ref, *rest, n_heads, group, scale):
    page_refs, (suf_ref, o_ref, carry_ref) = rest[:2 * group], rest[2 * group:]
    j = pl.program_id(1)
    rows = n_heads * SAMPLE_ROWS
    suffix = suf_ref[...]

    def process(k_refs, v_refs, mask, carry):
        zs = []
        for k_ref in k_refs:
            zt = []
            for h in range(n_heads):
                kh = k_ref[pl.ds(h, KEY_TILE, stride=n_heads), :].astype(BF16)
                s = lax.dot_general(q_ref[h], kh, (((1,), (1,)), ((), ())), preferred_element_type=F32)
                zt.append(s * scale + bias_ref[h])
            zs.append(jnp.concatenate(zt, axis=0))
        a, carry = _sb_group(jnp.concatenate(zs, axis=1), mask, suffix, carry)
        a = a.astype(BF16)
        outs = []
        for h in range(n_heads):
            rows_h = slice(h * SAMPLE_ROWS, (h + 1) * SAMPLE_ROWS)
            acc = None
            for t, v_ref in enumerate(v_refs):
                vh = v_ref[pl.ds(h, KEY_TILE, stride=n_heads), :].astype(BF16)
                part = jnp.dot(a[rows_h, t * KEY_TILE:(t + 1) * KEY_TILE], vh, preferred_element_type=F32)
                acc = part if acc is None else acc + part
            outs.append(acc)
        return jnp.concatenate(outs, axis=0), carry

    @pl.when(j == 0)
    def _():
        row = lax.broadcasted_iota(jnp.int32, (rows, KEY_TILE), 0) % SAMPLE_ROWS
        col = lax.broadcasted_iota(jnp.int32, (rows, KEY_TILE), 1)
        out, carry = process([kn_ref], [vn_ref], col < row, jnp.zeros((rows, KEY_TILE), F32))
        o_ref[...] = out
        carry_ref[...] = carry

    @pl.when(j > 0)
    def _():
        out, carry = process(page_refs[:group][::-1], page_refs[group:][::-1], None, carry_ref[...])
        o_ref[...] += out
        carry_ref[...] = carry


def _sb_sample(q, k_new, v_new, cache_k, cache_v, page_table, bias, suffix, layer, head_dim):
    bd, n_heads = q.shape[:2]
    n_pages = page_table.shape[1]
    assert cache_k.shape[2] == KEY_TILE * n_heads and head_dim == LANES
    page_rows = KEY_TILE * n_heads
    group = max(g for g in (16, 8, 4, 2, 1) if n_pages % g == 0)
    body = functools.partial(_sb_sample_body, n_heads=n_heads, group=group, scale=head_dim ** -0.5)

    def page_blk(g):
        def index(b, j, pt):
            return (layer, pt[b, n_pages - 1 - ((jnp.maximum(j, 1) - 1) * group + g)], 0, 0)
        return pl.BlockSpec((None, None, page_rows, head_dim), index)

    new_blk = pl.BlockSpec((None, page_rows, head_dim), lambda b, j, pt: (b, 0, 0))
    pages = [page_blk(g) for g in range(group)]
    return pl.pallas_call(
        body,
        grid_spec=pltpu.PrefetchScalarGridSpec(
            num_scalar_prefetch=1,
            grid=(bd, n_pages // group + 1),
            in_specs=[
                pl.BlockSpec(memory_space=pltpu.SMEM),
                pl.BlockSpec((None, n_heads, SAMPLE_ROWS, head_dim), lambda b, j, pt: (b, 0, 0, 0)),
                new_blk, new_blk, *pages, *pages,
                pl.BlockSpec((2 * KEY_TILE, 2 * KEY_TILE), lambda b, j, pt: (0, 0)),
            ],
            out_specs=pl.BlockSpec((None, n_heads * SAMPLE_ROWS, head_dim), lambda b, j, pt: (b, 0, 0)),
            scratch_shapes=[pltpu.VMEM((n_heads * SAMPLE_ROWS, KEY_TILE), F32)],
        ),
        out_shape=jax.ShapeDtypeStruct((bd, n_heads * SAMPLE_ROWS, head_dim), F32),
        compiler_params=_params("parallel", "arbitrary"),
        name="sb_sample",
    )(page_table, bias[layer], q, k_new, v_new, *([cache_k] * group), *([cache_v] * group), suffix)


def _conv_body(a_ref, g_ref, pre_ref, w_ref, b_ref, lg_ref, lb_ref, o_ref, st_ref, ext_ref, sh_ref,
               *, t, t_valid, width, chunk):
    base = CONV_PAD - (width - 1)
    ext_ref[pl.ds(base, width - 1), :] = pre_ref[...]
    n_chunks = t // chunk

    def glu(i, _):
        r = pl.ds(pl.multiple_of(i * chunk, 8), chunk)
        u = a_ref[r, :] * jax.nn.sigmoid(g_ref[r, :])
        ext_ref[pl.ds(pl.multiple_of(CONV_PAD + i * chunk, 8), chunk), :] = u
        return 0

    lax.fori_loop(0, n_chunks, glu, 0)
    st_ref[...] = ext_ref[pl.ds(CONV_PAD + t_valid - (width - 1), width - 1), :]

    def conv(i, _):
        t0 = pl.multiple_of(i * chunk, 8)
        win = ext_ref[pl.ds(t0, chunk + CONV_PAD), :]
        for r in range(1, 8):
            sh_ref[r] = win[r:r + chunk + CONV_PAD - 8]
        acc = jnp.zeros((chunk, a_ref.shape[-1]), F32) + b_ref[...]
        for w in range(width):
            q, r = divmod(base + w, 8)
            if r == 0:
                tap = ext_ref[pl.ds(pl.multiple_of(t0 + 8 * q, 8), chunk), :]
            else:
                tap = sh_ref[r, pl.ds(8 * q, chunk), :]
            acc = acc + tap * w_ref[pl.ds(w, 1), :]
        mean = jnp.mean(acc, axis=-1, keepdims=True)
        cen = acc - mean
        y = cen * lax.rsqrt(jnp.mean(cen * cen, axis=-1, keepdims=True) + EPS) * lg_ref[...] + lb_ref[...]
        o_ref[pl.ds(t0, chunk), :] = (y * jax.nn.sigmoid(y)).astype(o_ref.dtype)
        return 0

    lax.fori_loop(0, n_chunks, conv, 0)


def _conv_branch(p, col_blk, prefix, w, b, ln_g, ln_b, layer, n_seq, t, t_valid):
    width, c = w.shape[1:]
    chunk = _row_tile(t, 48)
    assert t % chunk == 0 and width - 1 <= CONV_PAD and 0 < t_valid <= t
    body = functools.partial(_conv_body, t=t, t_valid=t_valid, width=width, chunk=chunk)
    vec = pl.BlockSpec((None, 1, c), lambda s: (layer, 0, 0))
    return pl.pallas_call(
        body,
        grid=(n_seq,),
        in_specs=[
            pl.BlockSpec((t, c), lambda s: (s, col_blk)),
            pl.BlockSpec((t, c), lambda s: (s, col_blk + 1)),
            pl.BlockSpec((None, width - 1, c), lambda s: (s, 0, 0)),
            pl.BlockSpec((None, width, c), lambda s: (layer, 0, 0)),
            vec, vec, vec,
        ],
        out_specs=[
            pl.BlockSpec((t, c), lambda s: (s, 0)),
            pl.BlockSpec((None, width - 1, c), lambda s: (s, 0, 0)),
        ],
        out_shape=[
            jax.ShapeDtypeStruct((n_seq * t, c), BF16),
            jax.ShapeDtypeStruct((n_seq, width - 1, c), F32),
        ],
        scratch_shapes=[pltpu.VMEM((CONV_PAD + t, c), F32), pltpu.VMEM((8, chunk + CONV_PAD - 8, c), F32)],
        compiler_params=_params("parallel"),
        name="conv_branch",
    )(p, p, prefix, w, b, ln_g, ln_b)


def _lru_body(x_ref, gate_ref, pre_ref, h0_ref, cw_ref, cb_ref, wa_ref, ba_ref, wx_ref, bx_ref, lam_ref, og_ref,
              o_ref, st_ref, hl_ref, ext_ref, a_ref, u_ref, *, t, t_valid, width, chunk):
    base = LRU_PAD - (width - 1)
    ext_ref[pl.ds(base, width - 1), :] = pre_ref[...]
    n_chunks = t // chunk

    def fill(i, _):
        t0 = pl.multiple_of(i * chunk, 8)
        ext_ref[pl.ds(LRU_PAD + t0, chunk), :] = x_ref[pl.ds(t0, chunk), :]
        return 0

    lax.fori_loop(0, n_chunks, fill, 0)
    st_ref[...] = ext_ref[pl.ds(LRU_PAD + t_valid - (width - 1), width - 1), :]
    neg_c_sp = -LRU_C * _softplus(-lam_ref[...])

    def gates(i, _):
        t0 = pl.multiple_of(i * chunk, 8)
        win = ext_ref[pl.ds(t0, chunk + LRU_PAD), :]
        xc = jnp.zeros((chunk, x_ref.shape[-1]), F32) + cb_ref[...]
        for w in range(width):
            xc = xc + win[base + w:base + w + chunk] * cw_ref[pl.ds(w, 1), :]
        xb = xc.astype(BF16)
        r = jax.nn.sigmoid(jnp.dot(xb, wa_ref[...], preferred_element_type=F32) + ba_ref[...])
        ig = jax.nn.sigmoid(jnp.dot(xb, wx_ref[...], preferred_element_type=F32) + bx_ref[...])
        log_a = r * neg_c_sp
        a = jnp.exp(log_a)
        a_ref[pl.ds(t0, chunk), :] = a
        u_ref[pl.ds(t0, chunk), :] = jnp.sqrt(-jnp.tanh(log_a) * (1.0 + a * a)) * (ig * xc)
        return 0

    lax.fori_loop(0, n_chunks, gates, 0)

    def step(s, h):
        h = a_ref[pl.ds(s, 1), :] * h + u_ref[pl.ds(s, 1), :]
        u_ref[pl.ds(s, 1), :] = h
        return h

    hl_ref[...] = lax.fori_loop(0, t_valid, step, h0_ref[...], unroll=min(8, t_valid))

    def out(i, _):
        r = pl.ds(pl.multiple_of(i * chunk, 8), chunk)
        y = u_ref[r, :] * jax.nn.gelu(gate_ref[r, :])
        o_ref[r, :] = _rms(y, og_ref[...]).astype(o_ref.dtype)
        return 0

    lax.fori_loop(0, n_chunks, out, 0)


def _lru_branch(p, col_blk, prefix, h0, cw, cb, wa, ba, wx, bx, lam, out_g, layer, n_seq, t, t_valid, chunk):
    width, c = cw.shape[1:]
    assert t % chunk == 0 and chunk % 8 == 0 and width - 1 <= LRU_PAD
    body = functools.partial(_lru_body, t=t, t_valid=t_valid, width=width, chunk=chunk)
    vec = pl.BlockSpec((None, 1, c), lambda s: (layer, 0, 0))
    mat = pl.BlockSpec((None, c, c), lambda s: (layer, 0, 0))
    return pl.pallas_call(
        body,
        grid=(n_seq,),
        in_specs=[
            pl.BlockSpec((t, c), lambda s: (s, col_blk)),
            pl.BlockSpec((t, c), lambda s: (s, col_blk + 1)),
            pl.BlockSpec((None, width - 1, c), lambda s: (s, 0, 0)),
            pl.BlockSpec((None, 1, c), lambda s: (s, 0, 0)),
            pl.BlockSpec((None, width, c), lambda s: (layer, 0, 0)),
            vec, mat, vec, mat, vec, vec, vec,
        ],
        out_specs=[
            pl.BlockSpec((t, c), lambda s: (s, 0)),
            pl.BlockSpec((None, width - 1, c), lambda s: (s, 0, 0)),
            pl.BlockSpec((None, 1, c), lambda s: (s, 0, 0)),
        ],
        out_shape=[
            jax.ShapeDtypeStruct((n_seq * t, c), BF16),
            jax.ShapeDtypeStruct((n_seq, width - 1, c), F32),
            jax.ShapeDtypeStruct((n_seq, 1, c), F32),
        ],
        scratch_shapes=[pltpu.VMEM((LRU_PAD + t, c), F32), pltpu.VMEM((t, c), F32), pltpu.VMEM((t, c), F32)],
        compiler_params=_params("parallel"),
        name="lru_branch",
    )(p, p, prefix, h0, cw, cb, wa, ba, wx, bx, lam, out_g)


def _block_diag(w):
    depth, nb, bi, bj = w.shape
    eye = jnp.eye(nb, dtype=w.dtype)
    return jnp.einsum("lnij,nm->lnimj", w, eye).reshape(depth, nb * bi, nb * bj)


def kernel(x_prompt, x_sample, cache_k, cache_v, state_conv, state_lru_conv, state_lru_h, page_table, meta_tokens, norm_mix, w_in, sb_bias, conv_w, conv_b, conv_ln_g, conv_ln_b, lru_conv_w, lru_conv_b, lru_wa, lru_ba, lru_wx, lru_bx, lru_lambda, attn_out_g, lru_out_g, w_out, norm_ffn, w_gate, w_up, w_down, final_norm):
    bp, seq, d_model = x_prompt.shape
    bs, dec_seq, _ = x_sample.shape
    depth, n_pool, page, n_heads, head_dim = cache_k.shape
    n_meta = meta_tokens.shape[0]
    d_att = n_heads * head_dim
    d_conv = conv_w.shape[-1]
    d_lru = lru_conv_w.shape[-1]
    conv_width = conv_w.shape[1]
    lru_width = lru_conv_w.shape[1]
    t_p = n_meta + seq
    t_s = SAMPLE_ROWS
    assert dec_seq <= SAMPLE_ROWS and d_conv == d_lru and d_att % d_conv == 0
    assert w_in.shape[-1] == 3 * d_att + 2 * d_conv + 2 * d_lru

    w_in_b = w_in.astype(BF16)
    wa_b, wx_b = _block_diag(lru_wa).astype(BF16), _block_diag(lru_wx).astype(BF16)
    row = lambda v: v.reshape(depth, 1, -1)
    norm_mix_r, norm_ffn_r, attn_g_r, lru_g_r = row(norm_mix), row(norm_ffn), row(attn_out_g), row(lru_out_g)
    conv_b_r, ln_g_r, ln_b_r = row(conv_b), row(conv_ln_g), row(conv_ln_b)
    lcb_r, ba_r, bx_r, lam_r = row(lru_conv_b), row(lru_ba), row(lru_bx), row(lru_lambda)
    suffix = _suffix_matrix()
    cache_k2 = cache_k.reshape(depth, n_pool, page * n_heads, head_dim)
    cache_v2 = cache_v.reshape(depth, n_pool, page * n_heads, head_dim)

    xp = _prepend_meta(meta_tokens.astype(x_prompt.dtype), x_prompt).reshape(bp * t_p, d_model)
    xs = jnp.pad(x_sample, ((0, 0), (0, t_s - dec_seq), (0, 0))).reshape(bs * t_s, d_model)

    zero_conv = jnp.zeros((bp, conv_width - 1, d_conv), F32)
    zero_lru = jnp.zeros((bp, lru_width - 1, d_lru), F32)
    zero_h = jnp.zeros((bp, 1, d_lru), F32)
    conv_blk = 3 * d_att // d_conv
    lru_chunk_p = _row_tile(t_p, 688)

    d_in = w_in.shape[-1]

    def mixers(x, layer, n_seq, t, t_valid, conv_pre, lru_pre, h0, lru_chunk):
        p, k, v = _in_proj(x, norm_mix_r, w_in_b, layer, 2048, d_att)
        o_conv, st_conv = _conv_branch(p, conv_blk, conv_pre, conv_w, conv_b_r, ln_g_r, ln_b_r, layer, n_seq, t, t_valid)
        o_lru, st_lru, h_last = _lru_branch(p, conv_blk + 2, lru_pre, h0, lru_conv_w, lcb_r, wa_b, ba_r, wx_b, bx_r,
                                            lam_r, lru_g_r, layer, n_seq, t, t_valid, lru_chunk)
        return p, k, v, o_conv, o_lru, st_conv, st_lru, h_last

    def finish(x, layer, att, o_conv, o_lru, ffn_w):
        x = _mix_out(x, att, attn_g_r, o_conv, o_lru, ffn_w[0], layer, _col_tile(d_model, 1024))
        act = _norm_gateup(x, norm_ffn_r, ffn_w[1], ffn_w[2], layer, _col_tile(w_gate.shape[-1], 1408))
        return _res_matmul(x, act, ffn_w[3], 0, _col_tile(d_model, 1024))

    prompt_states, sample_states = [], []
    for layer in range(depth):
        p, k, v, o_conv, o_lru, st_conv, st_lru, h_last = mixers(
            xp, layer, bp, t_p, t_p, zero_conv, zero_lru, zero_h, lru_chunk_p)
        att, ffn_w = _sb_prompt(p, sb_bias, suffix, layer, bp, t_p, n_meta, n_heads, head_dim,
                                (w_out, w_gate, w_up, w_down))
        xp = finish(xp, layer, att, o_conv, o_lru, ffn_w)
        prompt_states.append((k.reshape(bp, t_p, n_heads, head_dim), v.reshape(bp, t_p, n_heads, head_dim),
                              st_conv, st_lru, h_last.reshape(bp, d_lru)))

        p, k, v, o_conv, o_lru, st_conv, st_lru, h_last = mixers(
            xs, layer, bs, t_s, dec_seq, state_conv[layer], state_lru_conv[layer],
            state_lru_h[layer].reshape(bs, 1, d_lru), t_s)
        q4 = p[:, :d_att].astype(BF16).reshape(bs, t_s, n_heads, head_dim).transpose(0, 2, 1, 3)
        pad_new = lambda a: jnp.pad(a.reshape(bs, t_s, d_att)[:, :dec_seq], ((0, 0), (0, KEY_TILE - dec_seq), (0, 0))
                                    ).reshape(bs, KEY_TILE * n_heads, head_dim)
        att = _sb_sample(q4, pad_new(k), pad_new(v), cache_k2, cache_v2, page_table, sb_bias, suffix, layer, head_dim)
        att = att.reshape(bs, n_heads, t_s, head_dim).transpose(0, 2, 1, 3).reshape(bs * t_s, d_att)
        xs = finish(xs, layer, att, o_conv, o_lru, ffn_w)
        new = lambda a: a.reshape(bs, t_s, n_heads, head_dim)[:, :dec_seq]
        sample_states.append((new(k), new(v), st_conv, st_lru, h_last.reshape(bs, d_lru)))

    stack = lambda states: [jnp.stack([s[i] for s in states], axis=0) for i in range(5)]
    kp, vp, cp, lcp, hp = stack(prompt_states)
    ksm, vsm, csm, lcsm, hsm = stack(sample_states)
    fin = final_norm.reshape(1, d_model)
    y_prompt = _final_norm(xp.reshape(bp, t_p, d_model)[:, n_meta:].reshape(bp * seq, d_model), fin)
    y_sample = _final_norm(xs, fin).reshape(bs, t_s, d_model)[:, :dec_seq]
    return (y_prompt.reshape(bp, seq, d_model), y_sample, kp, vp, cp, lcp, hp, ksm, vsm, csm, lcsm, hsm)
```
